```python
import jax
import jax.numpy as jnp
from jax import lax
import numpy as np

D_MODEL = 2048
BATCH = 2
SEQ = 4096
DEPTH = 4
DEC_BATCH = 32
DEC_SEQ = 64
PAST_LEN = 4096

CHUNK = 64
EPS = 1e-6
A_HEADS = 4
A_HEAD_DIM = 128
D_A = A_HEADS * A_HEAD_DIM
D_B = 512
CONV_WIDTH = 31
C_HEADS = 8
C_NOPE = 128
C_ROPE = 64
C_VDIM = 128
Q_RANK = 512
KV_RANK = 256
D_C = C_HEADS * C_VDIM
ROPE_BASE = 10000.0
ATTN_SCALE = (C_NOPE + C_ROPE) ** -0.5
Q_BLOCK = 128
SPLIT_SIZES = (D_A, D_A, D_A, D_A, A_HEADS, A_HEADS, 2 * D_B, Q_RANK, KV_RANK, C_ROPE)
SPLIT_IDX = tuple(int(i) for i in np.cumsum(SPLIT_SIZES)[:-1])
IN_COLS = int(sum(SPLIT_SIZES))
PEER_HEADS = 8
N_KEYS = 128
N_EXPERTS = N_KEYS * N_KEYS
D_KEY = 256
PEER_TOPK = 16
PEER_BLOCK = 128

kernel_name = "hymba_mlstm_conformer_mla_peer_streaming_step"


def rmsnorm(x, g):
    xf = x.astype(jnp.float32)
    y = xf * lax.rsqrt(jnp.mean(xf * xf, axis=-1, keepdims=True) + EPS)
    return (y * g.astype(jnp.float32)).astype(x.dtype)


def layernorm(x, g, b):
    xf = x.astype(jnp.float32)
    mu = jnp.mean(xf, axis=-1, keepdims=True)
    var = jnp.mean(jnp.square(xf - mu), axis=-1, keepdims=True)
    y = (xf - mu) * lax.rsqrt(var + EPS) * g.astype(jnp.float32) + b.astype(jnp.float32)
    return y.astype(x.dtype)


def rope(x, pos):
    half = C_ROPE // 2
    inv = ROPE_BASE ** (-jnp.arange(half, dtype=jnp.float32) / half)
    ang = pos.astype(jnp.float32)[:, None] * inv[None, :]
    shape = (1, ang.shape[0]) + (1,) * (x.ndim - 3) + (half,)
    cos = jnp.cos(ang).reshape(shape)
    sin = jnp.sin(ang).reshape(shape)
    xf = x.astype(jnp.float32)
    x1, x2 = xf[..., :half], xf[..., half:]
    return jnp.concatenate([x1 * cos - x2 * sin, x2 * cos + x1 * sin], axis=-1).astype(x.dtype)


def mlstm(q, k, v, ig, lf, C0, n0, m0, chunk_len):
    B, T, H, d = q.shape
    nc = T // chunk_len
    f32 = jnp.float32

    def to_chunks(a):
        return jnp.moveaxis(a.astype(f32).reshape((B, nc, chunk_len) + a.shape[2:]), 1, 0)

    causal = jnp.tril(jnp.ones((chunk_len, chunk_len), dtype=bool))

    def step(carry, xs):
        C, n, m = carry
        qc, kc, vc, igc, lfc = xs
        b = jnp.cumsum(lfc, axis=1)
        dmat = b[:, :, None, :] - b[:, None, :, :] + igc[:, None, :, :]
        dmat = jnp.where(causal[None, :, :, None], dmat, -jnp.inf)
        inter = b + m[:, None, :]
        m_t = jnp.maximum(inter, jnp.max(dmat, axis=2))
        w_intra = jnp.exp(dmat - m_t[:, :, None, :])
        w_inter = jnp.exp(inter - m_t)
        s = jnp.einsum('bthd,bshd->btsh', qc, kc) * w_intra
        num = jnp.einsum('btsh,bshv->bthv', s, vc) + w_inter[..., None] * jnp.einsum('bthd,bhdv->bthv', qc, C)
        den = jnp.sum(s, axis=2) + w_inter * jnp.einsum('bthd,bhd->bth', qc, n)
        h = num / jnp.maximum(jnp.abs(den), jnp.exp(-m_t))[..., None]
        m_new = m_t[:, -1]
        decay = jnp.exp(b[:, -1] + m - m_new)
        wk = jnp.exp(b[:, -1:, :] - b + igc - m_new[:, None, :])
        C_new = decay[..., None, None] * C + jnp.einsum('bsh,bshd,bshv->bhdv', wk, kc, vc)
        n_new = decay[..., None] * n + jnp.einsum('bsh,bshd->bhd', wk, kc)
        return (C_new, n_new, m_new), h

    carry0 = (C0.astype(f32), n0.astype(f32), m0.astype(f32))
    (C, n, m), h = lax.scan(step, carry0, (to_chunks(q), to_chunks(k), to_chunks(v), to_chunks(ig), to_chunks(lf)))
    h = jnp.moveaxis(h, 0, 1).reshape(B, T, H, d)
    return h, C, n, m


def causal_dwconv(u, buf, w, b):
    xp = jnp.concatenate([buf.astype(u.dtype), u], axis=1)
    y = lax.conv_general_dilated(xp, w.astype(u.dtype)[:, None, :], window_strides=(1,), padding='VALID',
                                 dimension_numbers=('NWC', 'WIO', 'NWC'), feature_group_count=u.shape[-1])
    return y + b.astype(u.dtype), xp[:, -(CONV_WIDTH - 1):]


def mla_attend_prompt(q_lat, q_rope, c_kv, k_rope, w_uv):
    B, S, H, _ = q_lat.shape
    nb = S // Q_BLOCK
    qb = jnp.moveaxis(q_lat.reshape(B, nb, Q_BLOCK, H, KV_RANK), 1, 0)
    qrb = jnp.moveaxis(q_rope.reshape(B, nb, Q_BLOCK, H, C_ROPE), 1, 0)
    kchunk = jnp.arange(S) // CHUNK

    def block(args):
        ql, qr, i = args
        qchunk = (i * Q_BLOCK + jnp.arange(Q_BLOCK)) // CHUNK
        s = (jnp.einsum('bqhc,bkc->bhqk', ql, c_kv) + jnp.einsum('bqhr,bkr->bhqk', qr, k_rope)).astype(jnp.float32) * ATTN_SCALE
        s = jnp.where(kchunk[None, :] <= qchunk[:, None], s, -jnp.inf)
        p = jax.nn.softmax(s, axis=-1).astype(c_kv.dtype)
        return jnp.einsum('bhqk,bkc->bqhc', p, c_kv)

    o = lax.map(block, (qb, qrb, jnp.arange(nb)))
    o = jnp.moveaxis(o, 0, 1).reshape(B, S, H, KV_RANK)
    return jnp.einsum('bshc,hcv->bshv', o, w_uv)


def mla_attend_sample(q_lat, q_rope, c_kv, k_rope, cache_lat, cache_rope, w_uv):
    keys_lat = jnp.concatenate([cache_lat.astype(c_kv.dtype), c_kv], axis=1)
    keys_rope = jnp.concatenate([cache_rope.astype(k_rope.dtype), k_rope], axis=1)
    s = (jnp.einsum('bqhc,bkc->bhqk', q_lat, keys_lat) + jnp.einsum('bqhr,bkr->bhqk', q_rope, keys_rope)).astype(jnp.float32) * ATTN_SCALE
    p = jax.nn.softmax(s, axis=-1).astype(c_kv.dtype)
    o = jnp.einsum('bhqk,bkc->bqhc', p, keys_lat)
    return jnp.einsum('bqhc,hcv->bqhv', o, w_uv)


def token_mixers(h, pos, C0, n0, m0, conv_buf, cache_lat, cache_rope, chunk_len,
                 w_in, b_gate, mlstm_norm_g, conv_w, conv_b, conv_ln_g, conv_ln_b,
                 q_norm_g, kv_norm_g, w_uq, w_uk, w_uv, w_out):
    B, T, _ = h.shape
    proj = h @ w_in
    q_a, k_a, v_a, o_a, ig, fg, glu_in, c_q, c_kv, k_r = jnp.split(proj, SPLIT_IDX, axis=-1)
    q_a = q_a.reshape(B, T, A_HEADS, A_HEAD_DIM)
    k_a = k_a.reshape(B, T, A_HEADS, A_HEAD_DIM) * (A_HEAD_DIM ** -0.5)
    v_a = v_a.reshape(B, T, A_HEADS, A_HEAD_DIM)
    bg = b_gate.astype(jnp.float32)
    ig = ig.astype(jnp.float32) + bg[:A_HEADS]
    lf = jax.nn.log_sigmoid(fg.astype(jnp.float32) + bg[A_HEADS:])
    h_a, C, n, m = mlstm(q_a, k_a, v_a, ig, lf, C0, n0, m0, chunk_len)
    h_a = rmsnorm(h_a.astype(h.dtype), mlstm_norm_g.reshape(A_HEADS, A_HEAD_DIM))
    y_a = jax.nn.sigmoid(o_a) * h_a.reshape(B, T, D_A)
    ga, gb = jnp.split(glu_in, 2, axis=-1)
    u = ga * jax.nn.sigmoid(gb)
    u_conv, new_buf = causal_dwconv(u, conv_buf, conv_w, conv_b)
    y_b = jax.nn.silu(layernorm(u_conv, conv_ln_g, conv_ln_b))
    cq = rmsnorm(c_q, q_norm_g)
    qh = (cq @ w_uq).reshape(B, T, C_HEADS, C_NOPE + C_ROPE)
    q_nope, q_rope = qh[..., :C_NOPE], rope(qh[..., C_NOPE:], pos)
    ckv = rmsnorm(c_kv, kv_norm_g)
    kr = rope(k_r, pos)
    q_lat = jnp.einsum('bthn,hcn->bthc', q_nope, w_uk)
    if cache_lat is None:
        y_c = mla_attend_prompt(q_lat, q_rope, ckv, kr, w_uv)
    else:
        y_c = mla_attend_sample(q_lat, q_rope, ckv, kr, cache_lat, cache_rope, w_uv)
    y_c = y_c.reshape(B, T, D_C)
    y = jnp.concatenate([y_a, y_b, y_c], axis=-1) @ w_out
    return y, C, n, m, new_buf, ckv, kr


def peer(x, w_q, sub_keys, u_tab, v_tab):
    B, T, D = x.shape
    n_tok = B * T
    xt = x.reshape(n_tok, D)
    n_pad = (-n_tok) % PEER_BLOCK
    xt = jnp.pad(xt, ((0, n_pad), (0, 0)))
    nb = xt.shape[0] // PEER_BLOCK

    def block(xb):
        q = (xb @ w_q).reshape(PEER_BLOCK, PEER_HEADS, 2, D_KEY // 2)
        s = jnp.einsum('phzc,hznc->phzn', q, sub_keys).astype(jnp.float32)
        v1, i1 = lax.top_k(s[:, :, 0], PEER_TOPK)
        v2, i2 = lax.top_k(s[:, :, 1], PEER_TOPK)
        cand = (v1[..., :, None] + v2[..., None, :]).reshape(PEER_BLOCK, PEER_HEADS, PEER_TOPK * PEER_TOPK)
        sc, j = lax.top_k(cand, PEER_TOPK)
        e = jnp.take_along_axis(i1, j // PEER_TOPK, axis=-1) * N_KEYS + jnp.take_along_axis(i2, j % PEER_TOPK, axis=-1)
        g = jax.nn.softmax(sc, axis=-1)
        u = jnp.take(u_tab, e, axis=0)
        a = jax.nn.gelu(jnp.einsum('pd,phkd->phk', xb, u).astype(jnp.float32), approximate=False)
        w = (g * a).astype(xb.dtype)
        return jnp.einsum('phk,phkd->pd', w, jnp.take(v_tab, e, axis=0))

    out = lax.map(block, xt.reshape(nb, PEER_BLOCK, D)).reshape(nb * PEER_BLOCK, D)[:n_tok]
    return out.reshape(B, T, D)


def setup_inputs(seed: int = 0) -> dict:
    key = jax.random.key(seed)
    ks = iter(jax.random.split(key, 40))

    def nrm(shape, scale):
        return scale * jax.random.normal(next(ks), shape, jnp.float32)

    b_gate = jnp.concatenate([nrm((DEPTH, A_HEADS), 0.1),
                              jnp.linspace(3.0, 6.0, A_HEADS, dtype=jnp.float32)[None, :] + nrm((DEPTH, A_HEADS), 0.1)], axis=-1)
    return {
        'x_prompt': nrm((BATCH, SEQ, D_MODEL), 1.0),
        'x_sample': nrm((DEC_BATCH, DEC_SEQ, D_MODEL), 1.0),
        'state_mlstm_C': nrm((DEPTH, DEC_BATCH, A_HEADS, A_HEAD_DIM, A_HEAD_DIM), 0.1),
        'state_mlstm_n': nrm((DEPTH, DEC_BATCH, A_HEADS, A_HEAD_DIM), 0.1),
        'state_mlstm_m': nrm((DEPTH, DEC_BATCH, A_HEADS), 1.0),
        'state_conv': nrm((DEPTH, DEC_BATCH, CONV_WIDTH - 1, D_B), 0.5),
        'cache_kv_latent': nrm((DEPTH, DEC_BATCH, PAST_LEN, KV_RANK), 1.0),
        'cache_k_rope': nrm((DEPTH, DEC_BATCH, PAST_LEN, C_ROPE), 1.0),
        'norm1_g': 1.0 + nrm((DEPTH, D_MODEL), 0.02),
        'w_in': nrm((DEPTH, D_MODEL, IN_COLS), D_MODEL ** -0.5),
        'b_gate': b_gate,
        'mlstm_norm_g': 1.0 + nrm((DEPTH, D_A), 0.02),
        'conv_w': nrm((DEPTH, CONV_WIDTH, D_B), CONV_WIDTH ** -0.5),
        'conv_b': nrm((DEPTH, D_B), 0.02),
        'conv_ln_g': 1.0 + nrm((DEPTH, D_B), 0.02),
        'conv_ln_b': nrm((DEPTH, D_B), 0.02),
        'q_norm_g': 1.0 + nrm((DEPTH, Q_RANK), 0.02),
        'kv_norm_g': 1.0 + nrm((DEPTH, KV_RANK), 0.02),
        'w_uq': nrm((DEPTH, Q_RANK, C_HEADS * (C_NOPE + C_ROPE)), Q_RANK ** -0.5),
        'w_uk': nrm((DEPTH, C_HEADS, KV_RANK, C_NOPE), KV_RANK ** -0.5),
        'w_uv': nrm((DEPTH, C_HEADS, KV_RANK, C_VDIM), KV_RANK ** -0.5),
        'w_out': nrm((DEPTH, D_MODEL, D_MODEL), 0.5 * D_MODEL ** -0.5),
        'norm2_g': 1.0 + nrm((DEPTH, D_MODEL), 0.02),
        'peer_wq': nrm((DEPTH, D_MODEL, PEER_HEADS * D_KEY), D_MODEL ** -0.5),
        'peer_keys': nrm((DEPTH, PEER_HEADS, 2, N_KEYS, D_KEY // 2), (D_KEY // 2) ** -0.5),
        'peer_u': nrm((DEPTH, N_EXPERTS, D_MODEL), D_MODEL ** -0.5),
        'peer_v': nrm((DEPTH, N_EXPERTS, D_MODEL), 0.05),
        'final_g': 1.0 + nrm((D_MODEL,), 0.02),
    }


def reference(x_prompt, x_sample, state_mlstm_C, state_mlstm_n, state_mlstm_m, state_conv,
              cache_kv_latent, cache_k_rope, norm1_g, w_in, b_gate, mlstm_norm_g, conv_w, conv_b,
              conv_ln_g, conv_ln_b, q_norm_g, kv_norm_g, w_uq, w_uk, w_uv, w_out, norm2_g,
              peer_wq, peer_keys, peer_u, peer_v, final_g):
    Bp, Sp, _ = x_prompt.shape
    Bs, Ts, _ = x_sample.shape
    pos_p = jnp.arange(Sp)
    pos_s = PAST_LEN + jnp.arange(Ts)
    xp, xs = x_prompt, x_sample
    p_st = [[] for _ in range(6)]
    s_st = [[] for _ in range(6)]
    for l in range(DEPTH):
        lw = (w_in[l], b_gate[l], mlstm_norm_g[l], conv_w[l], conv_b[l], conv_ln_g[l], conv_ln_b[l],
              q_norm_g[l], kv_norm_g[l], w_uq[l], w_uk[l], w_uv[l], w_out[l])
        pw = (peer_wq[l], peer_keys[l], peer_u[l], peer_v[l])
        C0 = jnp.zeros((Bp, A_HEADS, A_HEAD_DIM, A_HEAD_DIM), jnp.float32)
        n0 = jnp.zeros((Bp, A_HEADS, A_HEAD_DIM), jnp.float32)
        m0 = jnp.zeros((Bp, A_HEADS), jnp.float32)
        buf0 = jnp.zeros((Bp, CONV_WIDTH - 1, D_B), xp.dtype)
        yp, *sp = token_mixers(rmsnorm(xp, norm1_g[l]), pos_p, C0, n0, m0, buf0, None, None, CHUNK, *lw)
        xp = xp + yp
        xp = xp + peer(rmsnorm(xp, norm2_g[l]), *pw)
        ys, *ss = token_mixers(rmsnorm(xs, norm1_g[l]), pos_s, state_mlstm_C[l], state_mlstm_n[l], state_mlstm_m[l],
                               state_conv[l], cache_kv_latent[l], cache_k_rope[l], Ts, *lw)
        xs = xs + ys
        xs = xs + peer(rmsnorm(xs, norm2_g[l]), *pw)
        for i in range(6):
            p_st[i].append(sp[i])
            s_st[i].append(ss[i])
    y_prompt = rmsnorm(xp, final_g)
    y_sample = rmsnorm(xs, final_g)
    return (y_prompt, y_sample,
            jnp.stack(p_st[0]), jnp.stack(p_st[1]), jnp.stack(p_st[2]), jnp.stack(p_st[3]), jnp.stack(p_st[4]), jnp.stack(p_st[5]),
            jnp.stack(s_st[0]), jnp.stack(s_st[1]), jnp.stack(s_st[2]), jnp.stack(s_st[3]), jnp.stack(s_st[4]), jnp.stack(s_st[5]))
```

```python
import functools

import numpy as np
import jax
import jax.numpy as jnp
from jax import lax
from jax.experimental import pallas as pl
from jax.experimental.pallas import tpu as pltpu

F32 = jnp.float32
BF16 = jnp.bfloat16
HIGHEST = lax.Precision.HIGHEST

D_MODEL = 2048
DEPTH = 4
CHUNK = 64
EPS = 1e-6
A_HEADS = 4
A_HEAD_DIM = 128
D_A = A_HEADS * A_HEAD_DIM
D_B = 512
CONV_WIDTH = 31
C_HEADS = 8
C_NOPE = 128
C_ROPE = 64
C_VDIM = 128
Q_RANK = 512
KV_RANK = 256
D_C = C_HEADS * C_VDIM
ROPE_BASE = 10000.0
ATTN_SCALE = (C_NOPE + C_ROPE) ** -0.5
SPLIT_SIZES = (D_A, D_A, D_A, D_A, A_HEADS, A_HEADS, 2 * D_B, Q_RANK, KV_RANK, C_ROPE)
SPLIT_IDX = tuple(int(i) for i in np.cumsum(SPLIT_SIZES)[:-1])
IN_COLS = int(sum(SPLIT_SIZES))
PEER_HEADS = 8
N_KEYS = 128
N_EXPERTS = N_KEYS * N_KEYS
D_KEY = 256
PEER_TOPK = 16

PROJ_COLS = 4096
GATE_LANE = C_ROPE
CB_Q, CB_K, CB_V, CB_O = 0, 1, 2, 3
CB_GLU = 2
CB_CQ = 6
CB_CKV = 14
CB_KRG = 30

V7X_VMEM_LIMIT = 56 * 1024 * 1024
NEG_INF = float("-inf")


def _cparams(sem):
    return pltpu.CompilerParams(dimension_semantics=sem, vmem_limit_bytes=V7X_VMEM_LIMIT)


def _rms(x, g):
    return x * lax.rsqrt(jnp.mean(x * x, axis=-1, keepdims=True) + EPS) * g


def _log_sigmoid(x):
    return jnp.minimum(x, 0.0) - jnp.log1p(jnp.exp(-jnp.abs(x)))


def _norm_matmul_kernel(*refs, has_delta, emit_xn):
    it = iter(refs)
    x_ref = next(it)
    d_ref = next(it) if has_delta else None
    g_ref = next(it)
    w_ref = next(it)
    o_ref = next(it)
    xs_ref = next(it) if has_delta else None
    xn_out_ref = next(it) if emit_xn else None
    xn_ref = next(it)

    @pl.when(pl.program_id(1) == 0)
    def _():
        x = x_ref[...]
        if has_delta:
            x = x + d_ref[...]
            xs_ref[...] = x
        yb = _rms(x, g_ref[...]).astype(BF16)
        xn_ref[...] = yb
        if emit_xn:
            xn_out_ref[...] = yb

    o_ref[...] = jnp.dot(xn_ref[...], w_ref[...], preferred_element_type=F32)


def norm_matmul(x, delta, g, w, *, emit_xn=False, tm=512, tn=512):
    n, d = x.shape
    m = w.shape[1]
    has_delta = delta is not None
    row_spec = pl.BlockSpec((tm, d), lambda i, j: (i, 0))
    in_specs = [row_spec] + ([row_spec] if has_delta else []) + [
        pl.BlockSpec((1, d), lambda i, j: (0, 0)),
        pl.BlockSpec((d, tn), lambda i, j: (0, j)),
    ]
    out_shape = [jax.ShapeDtypeStruct((n, m), F32)]
    out_specs = [pl.BlockSpec((tm, tn), lambda i, j: (i, j))]
    if has_delta:
        out_shape.append(jax.ShapeDtypeStruct((n, d), F32))
        out_specs.append(row_spec)
    if emit_xn:
        out_shape.append(jax.ShapeDtypeStruct((n, d), BF16))
        out_specs.append(row_spec)
    args = [x] + ([delta] if has_delta else []) + [g.reshape(1, d), w]
    outs = pl.pallas_call(
        functools.partial(_norm_matmul_kernel, has_delta=has_delta, emit_xn=emit_xn),
        grid=(n // tm, m // tn),
        in_specs=in_specs,
        out_specs=out_specs,
        out_shape=out_shape,
        scratch_shapes=[pltpu.VMEM((tm, d), BF16)],
        compiler_params=_cparams(("parallel", "arbitrary")),
        name="norm_matmul",
    )(*args)
    outs = list(outs)
    proj = outs.pop(0)
    xs = outs.pop(0) if has_delta else x
    xn = outs.pop(0) if emit_xn else None
    return proj, xs, xn


def _mlstm_kernel(q_ref, k_ref, v_ref, o_ref, g_ref, c0_ref, n0_ref, m0_ref, bias_ref, ng_ref,
                  y_ref, c_ref, n_ref, m_ref):
    L = q_ref.shape[0]

    @pl.when(pl.program_id(1) == 0)
    def _():
        c_ref[...] = c0_ref[...]
        n_ref[...] = n0_ref[...]
        m_ref[...] = m0_ref[...]

    pre = g_ref[...] + bias_ref[...]
    lf_all = _log_sigmoid(pre)
    row = lax.broadcasted_iota(jnp.int32, (L, L), 0)
    col = lax.broadcasted_iota(jnp.int32, (L, L), 1)
    causal = col <= row
    tril = causal.astype(F32)
    triu = (row <= col).astype(F32)
    b_all = jnp.dot(tril, lf_all, precision=HIGHEST, preferred_element_type=F32)
    sel = (lax.broadcasted_iota(jnp.int32, (8, 128), 1)
           == lax.broadcasted_iota(jnp.int32, (8, 128), 0) + GATE_LANE).astype(F32)
    rows = lax.dot_general(sel, pre, (((1,), (1,)), ((), ())), precision=HIGHEST,
                           preferred_element_type=F32)
    ig_rows = rows[0:A_HEADS]
    lf_rows = _log_sigmoid(rows[A_HEADS:2 * A_HEADS])
    b_rows = jnp.dot(lf_rows, triu, precision=HIGHEST, preferred_element_type=F32)

    for h in range(A_HEADS):
        sl = slice(h * A_HEAD_DIM, (h + 1) * A_HEAD_DIM)
        q = q_ref[:, sl]
        k = k_ref[:, sl] * (A_HEAD_DIM ** -0.5)
        v = v_ref[:, sl]
        qb, kb, vb = q.astype(BF16), k.astype(BF16), v.astype(BF16)
        b_col = b_all[:, GATE_LANE + A_HEADS + h:GATE_LANE + A_HEADS + h + 1]
        ig_col = pre[:, GATE_LANE + h:GATE_LANE + h + 1]
        b_row = b_rows[h:h + 1]
        ig_row = ig_rows[h:h + 1]
        m_prev = m_ref[0, h:h + 1, 0:1]
        c_prev = c_ref[0, h]
        n_prev = n_ref[0, h:h + 1, :]

        dm = jnp.where(causal, b_col - b_row + ig_row, NEG_INF)
        inter = b_col + m_prev
        m_t = jnp.maximum(inter, jnp.max(dm, axis=1, keepdims=True))
        w_intra = jnp.exp(dm - m_t)
        w_inter = jnp.exp(inter - m_t)
        s = lax.dot_general(qb, kb, (((1,), (1,)), ((), ())), preferred_element_type=F32) * w_intra
        num = (jnp.dot(s.astype(BF16), vb, preferred_element_type=F32)
               + w_inter * jnp.dot(qb, c_prev.astype(BF16), preferred_element_type=F32))
        den = jnp.sum(s, axis=1, keepdims=True) + w_inter * jnp.sum(q * n_prev, axis=1, keepdims=True)
        hh = num / jnp.maximum(jnp.abs(den), jnp.exp(-m_t))

        m_new = m_t[L - 1:L, :]
        b_last = b_col[L - 1:L, :]
        decay = jnp.exp(b_last + m_prev - m_new)
        wk = jnp.exp(b_last - b_col + ig_col - m_new)
        kw = k * wk
        c_ref[0, h] = decay * c_prev + lax.dot_general(
            kw.astype(BF16), vb, (((0,), (0,)), ((), ())), preferred_element_type=F32)
        n_ref[0, h:h + 1, :] = decay * n_prev + jnp.sum(kw, axis=0, keepdims=True)
        m_ref[0, h:h + 1, :] = jnp.broadcast_to(m_new, (1, 128))

        hn = _rms(hh, ng_ref[:, sl])
        y_ref[:, sl] = (jax.nn.sigmoid(o_ref[:, sl]) * hn).astype(BF16)


def mlstm_group(proj, row0, bsz, t, chunk_len, c0, n0, m0, gate_bias, norm_g):
    nc = t // chunk_len
    rb0 = row0 // chunk_len

    def col_spec(width, cb):
        return pl.BlockSpec((chunk_len, width), lambda b, c: (rb0 + b * nc + c, cb))

    st4 = lambda b, c: (b, 0, 0, 0)
    st3 = lambda b, c: (b, 0, 0)
    m0p = jnp.pad(jnp.broadcast_to(m0[:, :, None], (bsz, A_HEADS, 128)), ((0, 0), (0, 8 - A_HEADS), (0, 0)))
    y, c_f, n_f, m_f = pl.pallas_call(
        _mlstm_kernel,
        grid=(bsz, nc),
        in_specs=[
            col_spec(D_A, CB_Q), col_spec(D_A, CB_K), col_spec(D_A, CB_V), col_spec(D_A, CB_O),
            col_spec(128, CB_KRG),
            pl.BlockSpec((1, A_HEADS, A_HEAD_DIM, A_HEAD_DIM), st4),
            pl.BlockSpec((1, A_HEADS, A_HEAD_DIM), st3),
            pl.BlockSpec((1, 8, 128), st3),
            pl.BlockSpec((1, 128), lambda b, c: (0, 0)),
            pl.BlockSpec((1, D_A), lambda b, c: (0, 0)),
        ],
        out_specs=[
            pl.BlockSpec((chunk_len, D_A), lambda b, c: (b * nc + c, 0)),
            pl.BlockSpec((1, A_HEADS, A_HEAD_DIM, A_HEAD_DIM), st4),
            pl.BlockSpec((1, A_HEADS, A_HEAD_DIM), st3),
            pl.BlockSpec((1, 8, 128), st3),
        ],
        out_shape=[
            jax.ShapeDtypeStruct((bsz * t, D_A), BF16),
            jax.ShapeDtypeStruct((bsz, A_HEADS, A_HEAD_DIM, A_HEAD_DIM), F32),
            jax.ShapeDtypeStruct((bsz, A_HEADS, A_HEAD_DIM), F32),
            jax.ShapeDtypeStruct((bsz, 8, 128), F32),
        ],
        compiler_params=_cparams(("parallel", "arbitrary")),
        name="mlstm",
    )(proj, proj, proj, proj, proj, c0, n0, m0p, gate_bias, norm_g.reshape(1, D_A))
    return y, c_f, n_f, m_f[:, :A_HEADS, 0]


CONV_HALO = 32
CONV_ROWS = 32


def _conv_kernel(glu_ref, buf_ref, w_ref, b_ref, lng_ref, lnb_ref, y_ref, nbuf_ref, xp_ref):
    tc = glu_ref.shape[0]
    j = pl.program_id(1)
    hist = CONV_WIDTH - 1

    @pl.when(j == 0)
    def _():
        xp_ref[0:CONV_HALO - hist, :] = jnp.zeros((CONV_HALO - hist, D_B), F32)
        xp_ref[CONV_HALO - hist:CONV_HALO, :] = buf_ref[0]

    @pl.when(j > 0)
    def _():
        xp_ref[0:CONV_HALO, :] = xp_ref[tc:tc + CONV_HALO, :]

    u = glu_ref[:, 0:D_B] * jax.nn.sigmoid(glu_ref[:, D_B:2 * D_B])
    xp_ref[CONV_HALO:CONV_HALO + tc, :] = u

    @pl.when(j == pl.num_programs(1) - 1)
    def _():
        nbuf_ref[0] = xp_ref[CONV_HALO + tc - hist:CONV_HALO + tc, :]

    for r0 in range(0, tc, CONV_ROWS):
        acc = jnp.broadcast_to(b_ref[...], (CONV_ROWS, D_B))
        for tap in range(CONV_WIDTH):
            start = CONV_HALO - hist + r0 + tap
            acc = acc + w_ref[tap:tap + 1, :] * xp_ref[start:start + CONV_ROWS, :]
        mu = jnp.mean(acc, axis=-1, keepdims=True)
        cen = acc - mu
        var = jnp.mean(cen * cen, axis=-1, keepdims=True)
        z = cen * lax.rsqrt(var + EPS) * lng_ref[...] + lnb_ref[...]
        y_ref[r0:r0 + CONV_ROWS, :] = (z * jax.nn.sigmoid(z)).astype(BF16)


def conv_group(proj, row0, bsz, t, tc, buf, w, b, ln_g, ln_b):
    nt = t // tc
    rb0 = row0 // tc
    const2 = lambda bb, j: (0, 0)
    y, nbuf = pl.pallas_call(
        _conv_kernel,
        grid=(bsz, nt),
        in_specs=[
            pl.BlockSpec((tc, 2 * D_B), lambda bb, j: (rb0 + bb * nt + j, CB_GLU)),
            pl.BlockSpec((1, CONV_WIDTH - 1, D_B), lambda bb, j: (bb, 0, 0)),
            pl.BlockSpec((CONV_WIDTH, D_B), const2),
            pl.BlockSpec((1, D_B), const2),
            pl.BlockSpec((1, D_B), const2),
            pl.BlockSpec((1, D_B), const2),
        ],
        out_specs=[
            pl.BlockSpec((tc, D_B), lambda bb, j: (bb * nt + j, 0)),
            pl.BlockSpec((1, CONV_WIDTH - 1, D_B), lambda bb, j: (bb, 0, 0)),
        ],
        out_shape=[
            jax.ShapeDtypeStruct((bsz * t, D_B), BF16),
            jax.ShapeDtypeStruct((bsz, CONV_WIDTH - 1, D_B), F32),
        ],
        scratch_shapes=[pltpu.VMEM((CONV_HALO + tc + CONV_HALO, D_B), F32)],
        compiler_params=_cparams(("parallel", "arbitrary")),
        name="conv_module",
    )(proj, buf, w, b.reshape(1, D_B), ln_g.reshape(1, D_B), ln_b.reshape(1, D_B))
    return y, nbuf


def _mla_prep_kernel(cq_ref, ckv_ref, krg_ref, cos_ref, sin_ref, qg_ref, kvg_ref, wuq_ref, wuk_ref,
                     qlat_ref, qrope_ref, ckv_out_ref, kr_out_ref):
    half = C_ROPE // 2
    cq = _rms(cq_ref[...], qg_ref[...]).astype(BF16)
    qh = jnp.dot(cq, wuq_ref[...], preferred_element_type=F32)
    cos = cos_ref[...]
    sin = sin_ref[...]
    x1 = qh[:, D_C:D_C + C_HEADS * half]
    x2 = qh[:, D_C + C_HEADS * half:D_C + 2 * C_HEADS * half]
    r1 = (x1 * cos - x2 * sin) * ATTN_SCALE
    r2 = (x2 * cos + x1 * sin) * ATTN_SCALE
    for h in range(C_HEADS):
        qn = qh[:, h * C_NOPE:(h + 1) * C_NOPE].astype(BF16)
        ql = lax.dot_general(qn, wuk_ref[h], (((1,), (1,)), ((), ())), preferred_element_type=F32)
        qlat_ref[h] = (ql * ATTN_SCALE).astype(BF16)
        qrope_ref[h] = jnp.concatenate(
            [r1[:, h * half:(h + 1) * half], r2[:, h * half:(h + 1) * half]], axis=1).astype(BF16)
    ckv_out_ref[...] = _rms(ckv_ref[...], kvg_ref[...])
    k1 = krg_ref[:, 0:half]
    k2 = krg_ref[:, half:C_ROPE]
    c1 = cos[:, 0:half]
    s1 = sin[:, 0:half]
    kr_out_ref[...] = jnp.concatenate([k1 * c1 - k2 * s1, k2 * c1 + k1 * s1], axis=1)


def mla_prep(proj, cos_t, sin_t, q_norm_g, kv_norm_g, w_uq_r, w_uk_b, *, tq=256):
    n = proj.shape[0]
    const2 = lambda i: (0, 0)
    return pl.pallas_call(
        _mla_prep_kernel,
        grid=(n // tq,),
        in_specs=[
            pl.BlockSpec((tq, Q_RANK), lambda i: (i, CB_CQ)),
            pl.BlockSpec((tq, KV_RANK), lambda i: (i, CB_CKV)),
            pl.BlockSpec((tq, 128), lambda i: (i, CB_KRG)),
            pl.BlockSpec((tq, C_HEADS * C_ROPE // 2), lambda i: (i, 0)),
            pl.BlockSpec((tq, C_HEADS * C_ROPE // 2), lambda i: (i, 0)),
            pl.BlockSpec((1, Q_RANK), const2),
            pl.BlockSpec((1, KV_RANK), const2),
            pl.BlockSpec((Q_RANK, C_HEADS * (C_NOPE + C_ROPE)), const2),
            pl.BlockSpec((C_HEADS, KV_RANK, C_NOPE), lambda i: (0, 0, 0)),
        ],
        out_specs=[
            pl.BlockSpec((C_HEADS, tq, KV_RANK), lambda i: (0, i, 0)),
            pl.BlockSpec((C_HEADS, tq, C_ROPE), lambda i: (0, i, 0)),
            pl.BlockSpec((tq, KV_RANK), lambda i: (i, 0)),
            pl.BlockSpec((tq, C_ROPE), lambda i: (i, 0)),
        ],
        out_shape=[
            jax.ShapeDtypeStruct((C_HEADS, n, KV_RANK), BF16),
            jax.ShapeDtypeStruct((C_HEADS, n, C_ROPE), BF16),
            jax.ShapeDtypeStruct((n, KV_RANK), F32),
            jax.ShapeDtypeStruct((n, C_ROPE), F32),
        ],
        compiler_params=_cparams(("parallel",)),
        name="mla_prep",
    )(proj, proj, proj, cos_t, sin_t, q_norm_g.reshape(1, Q_RANK), kv_norm_g.reshape(1, KV_RANK), w_uq_r, w_uk_b)


def _attn_block(q, qr, kc, krc, m_ref, l_ref, acc_ref, mask):
    kcb = kc.astype(BF16)
    s = (lax.dot_general(q, kcb, (((1,), (1,)), ((), ())), preferred_element_type=F32)
         + lax.dot_general(qr, krc.astype(BF16), (((1,), (1,)), ((), ())), preferred_element_type=F32))
    if mask is not None:
        s = jnp.where(mask, s, NEG_INF)
    m_prev = m_ref[...]
    m_new = jnp.maximum(m_prev, jnp.max(s, axis=1, keepdims=True))
    alpha = jnp.exp(m_prev - m_new)
    p = jnp.exp(s - m_new)
    l_ref[...] = alpha * l_ref[...] + jnp.sum(p, axis=1, keepdims=True)
    acc_ref[...] = alpha * acc_ref[...] + jnp.dot(p.astype(BF16), kcb, preferred_element_type=F32)
    m_ref[...] = m_new


def _attn_finish(acc_ref, l_ref, wuv_ref, y_ref, tq):
    o = acc_ref[...] / l_ref[...]
    for h in range(C_HEADS):
        oh = o[h * tq:(h + 1) * tq].astype(BF16)
        y_ref[:, h * C_VDIM:(h + 1) * C_VDIM] = jnp.dot(
            oh, wuv_ref[h], preferred_element_type=F32).astype(BF16)


def _attn_init(m_ref, l_ref, acc_ref):
    m_ref[...] = jnp.full(m_ref.shape, NEG_INF, F32)
    l_ref[...] = jnp.zeros(l_ref.shape, F32)
    acc_ref[...] = jnp.zeros(acc_ref.shape, F32)


def _attn_prompt_kernel(ql_ref, qr_ref, k_ref, kr_ref, wuv_ref, y_ref, m_ref, l_ref, acc_ref):
    tq = ql_ref.shape[1]
    i = pl.program_id(1)
    q = ql_ref[...].reshape(C_HEADS * tq, KV_RANK)
    qr = qr_ref[...].reshape(C_HEADS * tq, C_ROPE)
    _attn_init(m_ref, l_ref, acc_ref)

    def body(j, carry):
        start = pl.multiple_of(j * tq, tq)
        _attn_block(q, qr, k_ref[pl.ds(start, tq), :], kr_ref[pl.ds(start, tq), :], m_ref, l_ref, acc_ref, None)
        return carry

    lax.fori_loop(0, i, body, 0)
    start = pl.multiple_of(i * tq, tq)
    qtok = lax.broadcasted_iota(jnp.int32, (C_HEADS, tq, tq), 1).reshape(C_HEADS * tq, tq)
    ktok = lax.broadcasted_iota(jnp.int32, (C_HEADS * tq, tq), 1)
    mask = (ktok // CHUNK) <= (qtok // CHUNK)
    _attn_block(q, qr, k_ref[pl.ds(start, tq), :], kr_ref[pl.ds(start, tq), :], m_ref, l_ref, acc_ref, mask)
    _attn_finish(acc_ref, l_ref, wuv_ref, y_ref, tq)


def attn_prompt(q_lat, q_rope, ckv, kr, w_uv_b, bsz, t, *, tq=256):
    nq = t // tq
    rows = C_HEADS * tq
    return pl.pallas_call(
        _attn_prompt_kernel,
        grid=(bsz, nq),
        in_specs=[
            pl.BlockSpec((C_HEADS, tq, KV_RANK), lambda b, i: (0, b * nq + i, 0)),
            pl.BlockSpec((C_HEADS, tq, C_ROPE), lambda b, i: (0, b * nq + i, 0)),
            pl.BlockSpec((t, KV_RANK), lambda b, i: (b, 0)),
            pl.BlockSpec((t, C_ROPE), lambda b, i: (b, 0)),
            pl.BlockSpec((C_HEADS, KV_RANK, C_VDIM), lambda b, i: (0, 0, 0)),
        ],
        out_specs=pl.BlockSpec((tq, D_C), lambda b, i: (b * nq + i, 0)),
        out_shape=jax.ShapeDtypeStruct((bsz * t, D_C), BF16),
        scratch_shapes=[pltpu.VMEM((rows, 1), F32), pltpu.VMEM((rows, 1), F32), pltpu.VMEM((rows, KV_RANK), F32)],
        compiler_params=_cparams(("parallel", "arbitrary")),
        name="attn_prompt",
    )(q_lat, q_rope, ckv, kr, w_uv_b)


def _attn_sample_kernel(ql_ref, qr_ref, k_ref, kr_ref, ck_ref, ckr_ref, wuv_ref, y_ref, m_ref, l_ref, acc_ref,
                        *, kb):
    tq = ql_ref.shape[1]
    q = ql_ref[...].reshape(C_HEADS * tq, KV_RANK)
    qr = qr_ref[...].reshape(C_HEADS * tq, C_ROPE)
    _attn_init(m_ref, l_ref, acc_ref)
    past = ck_ref.shape[1]

    def body(j, carry):
        start = pl.multiple_of(j * kb, kb)
        _attn_block(q, qr, ck_ref[0, pl.ds(start, kb), :], ckr_ref[0, pl.ds(start, kb), :],
                    m_ref, l_ref, acc_ref, None)
        return carry

    lax.fori_loop(0, past // kb, body, 0)
    _attn_block(q, qr, k_ref[...], kr_ref[...], m_ref, l_ref, acc_ref, None)
    _attn_finish(acc_ref, l_ref, wuv_ref, y_ref, tq)


def attn_sample(q_lat, q_rope, ckv, kr, cache_lat, cache_rope, w_uv_b, row0, bsz, t, *, kb=512):
    rb0 = row0 // t
    past = cache_lat.shape[1]
    rows = C_HEADS * t
    return pl.pallas_call(
        functools.partial(_attn_sample_kernel, kb=kb),
        grid=(bsz,),
        in_specs=[
            pl.BlockSpec((C_HEADS, t, KV_RANK), lambda b: (0, rb0 + b, 0)),
            pl.BlockSpec((C_HEADS, t, C_ROPE), lambda b: (0, rb0 + b, 0)),
            pl.BlockSpec((t, KV_RANK), lambda b: (rb0 + b, 0)),
            pl.BlockSpec((t, C_ROPE), lambda b: (rb0 + b, 0)),
            pl.BlockSpec((1, past, KV_RANK), lambda b: (b, 0, 0)),
            pl.BlockSpec((1, past, C_ROPE), lambda b: (b, 0, 0)),
            pl.BlockSpec((C_HEADS, KV_RANK, C_VDIM), lambda b: (0, 0, 0)),
        ],
        out_specs=pl.BlockSpec((t, D_C), lambda b: (b, 0)),
        out_shape=jax.ShapeDtypeStruct((bsz * t, D_C), BF16),
        scratch_shapes=[pltpu.VMEM((rows, 1), F32), pltpu.VMEM((rows, 1), F32), pltpu.VMEM((rows, KV_RANK), F32)],
        compiler_params=_cparams(("parallel",)),
        name="attn_sample",
    )(q_lat, q_rope, ckv, kr, cache_lat, cache_rope, w_uv_b)


def _oproj_kernel(x_ref, ya_ref, yb_ref, yc_ref, wa_ref, wb_ref, wc_ref, o_ref):
    acc = x_ref[...]
    acc = acc + jnp.dot(ya_ref[...], wa_ref[...], preferred_element_type=F32)
    acc = acc + jnp.dot(yb_ref[...], wb_ref[...], preferred_element_type=F32)
    acc = acc + jnp.dot(yc_ref[...], wc_ref[...], preferred_element_type=F32)
    o_ref[...] = acc


def out_proj(x, ya, yb, yc, w_out_b, *, tm=1024, tn=512):
    n, d = x.shape
    return pl.pallas_call(
        _oproj_kernel,
        grid=(n // tm, d // tn),
        in_specs=[
            pl.BlockSpec((tm, tn), lambda i, j: (i, j)),
            pl.BlockSpec((tm, D_A), lambda i, j: (i, 0)),
            pl.BlockSpec((tm, D_B), lambda i, j: (i, 0)),
            pl.BlockSpec((tm, D_C), lambda i, j: (i, 0)),
            pl.BlockSpec((D_A, tn), lambda i, j: (0, j)),
            pl.BlockSpec((D_B, tn), lambda i, j: (1, j)),
            pl.BlockSpec((D_C, tn), lambda i, j: (1, j)),
        ],
        out_specs=pl.BlockSpec((tm, tn), lambda i, j: (i, j)),
        out_shape=jax.ShapeDtypeStruct((n, d), F32),
        compiler_params=_cparams(("parallel", "arbitrary")),
        name="out_proj",
    )(x, ya, yb, yc, w_out_b, w_out_b, w_out_b)


def _extract_max(cur):
    m = jnp.max(cur, axis=0, keepdims=True)
    return m, jnp.where(cur == m, NEG_INF, cur)


def _peer_select_kernel(q_ref, keys_ref, s2_ref, e2_ref, thr_ref, e1_ref):
    for h in range(PEER_HEADS):
        st = []
        top = []
        for z in range(2):
            qhz = q_ref[:, (2 * h + z) * N_KEYS:(2 * h + z + 1) * N_KEYS].astype(BF16)
            s = lax.dot_general(keys_ref[h, z], qhz, (((1,), (1,)), ((), ())),
                                preferred_element_type=F32)
            st.append(s)
            cur = s
            vals = []
            for _ in range(PEER_TOPK):
                m, cur = _extract_max(cur)
                vals.append(m)
            top.append(vals)
        v2 = jnp.concatenate(top[1], axis=0)
        cand = jnp.concatenate([top[0][a] + v2 for a in range(PEER_TOPK)], axis=0)
        cur = cand
        best, cur = _extract_max(cur)
        zsum = jnp.ones_like(best)
        m = best
        for _ in range(PEER_TOPK - 1):
            m, cur = _extract_max(cur)
            zsum = zsum + jnp.exp(m - best)
        nxt = jnp.max(cur, axis=0, keepdims=True)
        tau = 0.5 * (m + nxt)
        s2_ref[h] = st[1]
        e2_ref[h] = jnp.exp(st[1] - top[1][0]) / zsum
        thr_ref[h] = tau - st[0]
        e1_ref[h] = jnp.exp(st[0] - top[0][0])


def peer_select(q, keys_b, *, tp=256):
    n = q.shape[0]
    shp = jax.ShapeDtypeStruct((PEER_HEADS, N_KEYS, n), F32)
    spec = pl.BlockSpec((PEER_HEADS, N_KEYS, tp), lambda i: (0, 0, i))
    return pl.pallas_call(
        _peer_select_kernel,
        grid=(n // tp,),
        in_specs=[
            pl.BlockSpec((tp, PEER_HEADS * D_KEY), lambda i: (i, 0)),
            pl.BlockSpec((PEER_HEADS, 2, N_KEYS, D_KEY // 2), lambda i: (0, 0, 0, 0)),
        ],
        out_specs=[spec, spec, spec, spec],
        out_shape=[shp, shp, shp, shp],
        compiler_params=_cparams(("parallel",)),
        name="peer_select",
    )(q, keys_b)


def _peer_dense_kernel(xn_ref, s2_ref, e2_ref, thr_ref, e1_ref, u_ref, v_ref, o_ref, wt_ref, *, eb):
    j = pl.program_id(1)

    @pl.when(j == 0)
    def _():
        o_ref[...] = jnp.zeros(o_ref.shape, F32)

    at = lax.dot_general(u_ref[...], xn_ref[...], (((1,), (1,)), ((), ())),
                         preferred_element_type=F32)
    for ii in range(eb // N_KEYS):
        i = j * (eb // N_KEYS) + ii
        gate = None
        for h in range(PEER_HEADS):
            thr_row = thr_ref[h, pl.ds(i, 1), :]
            e1_row = e1_ref[h, pl.ds(i, 1), :]
            term = jnp.where(s2_ref[h] >= thr_row, e2_ref[h], 0.0) * e1_row
            gate = term if gate is None else gate + term
        a = at[ii * N_KEYS:(ii + 1) * N_KEYS]
        act = 0.5 * a * (1.0 + lax.erf(a * (2.0 ** -0.5)))
        wt_ref[ii * N_KEYS:(ii + 1) * N_KEYS, :] = (gate * act).astype(BF16)
    o_ref[...] += lax.dot_general(wt_ref[...], v_ref[...], (((0,), (0,)), ((), ())),
                                  preferred_element_type=F32)


def peer_dense(xn, s2, e2, thr, e1, u_b, v_b, *, tp=512, eb=512):
    n, d = xn.shape
    gspec = pl.BlockSpec((PEER_HEADS, N_KEYS, tp), lambda i, j: (0, 0, i))
    return pl.pallas_call(
        functools.partial(_peer_dense_kernel, eb=eb),
        grid=(n // tp, N_EXPERTS // eb),
        in_specs=[
            pl.BlockSpec((tp, d), lambda i, j: (i, 0)),
            gspec, gspec, gspec, gspec,
            pl.BlockSpec((eb, d), lambda i, j: (j, 0)),
            pl.BlockSpec((eb, d), lambda i, j: (j, 0)),
        ],
        out_specs=pl.BlockSpec((tp, d), lambda i, j: (i, 0)),
        out_shape=jax.ShapeDtypeStruct((n, d), F32),
        scratch_shapes=[pltpu.VMEM((eb, tp), BF16)],
        compiler_params=_cparams(("parallel", "arbitrary")),
        name="peer_dense",
    )(xn, s2, e2, thr, e1, u_b, v_b)


def _final_norm_kernel(x_ref, d_ref, g_ref, o_ref):
    o_ref[...] = _rms(x_ref[...] + d_ref[...], g_ref[...])


def final_norm(x, delta, g, *, tm=512):
    n, d = x.shape
    spec = pl.BlockSpec((tm, d), lambda i: (i, 0))
    return pl.pallas_call(
        _final_norm_kernel,
        grid=(n // tm,),
        in_specs=[spec, spec, pl.BlockSpec((1, d), lambda i: (0, 0))],
        out_specs=spec,
        out_shape=jax.ShapeDtypeStruct((n, d), F32),
        compiler_params=_cparams(("parallel",)),
        name="final_norm",
    )(x, delta, g.reshape(1, d))


def _prep_w_in(w_in):
    qa, ka, va, oa, ig, fg, glu, cq, ckv, kr = jnp.split(w_in, SPLIT_IDX, axis=-1)
    pad = jnp.zeros(w_in.shape[:-1] + (PROJ_COLS - IN_COLS,), w_in.dtype)
    return jnp.concatenate([qa, ka, va, oa, glu, cq, ckv, kr, ig, fg, pad], axis=-1).astype(BF16)


def _prep_w_uq(w_uq):
    w = w_uq.reshape(w_uq.shape[:-1] + (C_HEADS, C_NOPE + C_ROPE))
    half = C_ROPE // 2
    flat = lambda a: a.reshape(a.shape[:-2] + (-1,))
    return jnp.concatenate(
        [flat(w[..., :C_NOPE]), flat(w[..., C_NOPE:C_NOPE + half]), flat(w[..., C_NOPE + half:])],
        axis=-1).astype(BF16)


def _rope_tables(pos):
    half = C_ROPE // 2
    inv = ROPE_BASE ** (-jnp.arange(half, dtype=jnp.float32) / half)
    ang = pos.astype(jnp.float32)[:, None] * inv[None, :]
    return jnp.tile(jnp.cos(ang), (1, C_HEADS)), jnp.tile(jnp.sin(ang), (1, C_HEADS))


def kernel(x_prompt, x_sample, state_mlstm_C, state_mlstm_n, state_mlstm_m, state_conv, cache_kv_latent, cache_k_rope, norm1_g, w_in, b_gate, mlstm_norm_g, conv_w, conv_b, conv_ln_g, conv_ln_b, q_norm_g, kv_norm_g, w_uq, w_uk, w_uv, w_out, norm2_g, peer_wq, peer_keys, peer_u, peer_v, final_g):
    bp, sp, _ = x_prompt.shape
    bs, ts, _ = x_sample.shape
    past = cache_kv_latent.shape[2]
    n_p = bp * sp
    n_s = bs * ts

    x = jnp.concatenate([x_prompt.reshape(n_p, D_MODEL), x_sample.reshape(n_s, D_MODEL)], axis=0)
    pos = jnp.concatenate([jnp.tile(jnp.arange(sp), bp), jnp.tile(past + jnp.arange(ts), bs)])
    cos_t, sin_t = _rope_tables(pos)

    w_in_b = _prep_w_in(w_in)
    w_uq_b = _prep_w_uq(w_uq)
    w_uk_b = w_uk.astype(BF16)
    w_uv_b = w_uv.astype(BF16)
    w_out_b = w_out.astype(BF16)
    peer_wq_b = peer_wq.astype(BF16)
    peer_keys_b = peer_keys.astype(BF16)
    peer_u_b = peer_u.astype(BF16)
    peer_v_b = peer_v.astype(BF16)
    gate_bias = jnp.pad(b_gate, ((0, 0), (GATE_LANE, 128 - GATE_LANE - 2 * A_HEADS)))[:, None, :]

    zc = jnp.zeros((bp, A_HEADS, A_HEAD_DIM, A_HEAD_DIM), F32)
    zn = jnp.zeros((bp, A_HEADS, A_HEAD_DIM), F32)
    zm = jnp.zeros((bp, A_HEADS), F32)
    zbuf = jnp.zeros((bp, CONV_WIDTH - 1, D_B), F32)

    p_st = [[] for _ in range(6)]
    s_st = [[] for _ in range(6)]
    delta = None
    for l in range(DEPTH):
        proj, x, _ = norm_matmul(x, delta, norm1_g[l], w_in_b[l])
        ya_p, c_p, nn_p, m_p = mlstm_group(proj, 0, bp, sp, CHUNK, zc, zn, zm, gate_bias[l], mlstm_norm_g[l])
        ya_s, c_s, nn_s, m_s = mlstm_group(proj, n_p, bs, ts, ts, state_mlstm_C[l], state_mlstm_n[l],
                                           state_mlstm_m[l], gate_bias[l], mlstm_norm_g[l])
        yb_p, buf_p = conv_group(proj, 0, bp, sp, 256, zbuf, conv_w[l], conv_b[l], conv_ln_g[l], conv_ln_b[l])
        yb_s, buf_s = conv_group(proj, n_p, bs, ts, ts, state_conv[l], conv_w[l], conv_b[l], conv_ln_g[l],
                                 conv_ln_b[l])
        q_lat, q_rope, ckv, kr = mla_prep(proj, cos_t, sin_t, q_norm_g[l], kv_norm_g[l], w_uq_b[l], w_uk_b[l])
        yc_p = attn_prompt(q_lat, q_rope, ckv, kr, w_uv_b[l], bp, sp)
        yc_s = attn_sample(q_lat, q_rope, ckv, kr, cache_kv_latent[l], cache_k_rope[l], w_uv_b[l], n_p, bs, ts)
        ya = jnp.concatenate([ya_p, ya_s], axis=0)
        yb = jnp.concatenate([yb_p, yb_s], axis=0)
        yc = jnp.concatenate([yc_p, yc_s], axis=0)
        x = out_proj(x, ya, yb, yc, w_out_b[l])
        q, _, xn = norm_matmul(x, None, norm2_g[l], peer_wq_b[l], emit_xn=True)
        s2, e2, thr, e1 = peer_select(q, peer_keys_b[l])
        delta = peer_dense(xn, s2, e2, thr, e1, peer_u_b[l], peer_v_b[l])

        for lst, vals in ((p_st, (c_p, nn_p, m_p, buf_p, ckv[:n_p].reshape(bp, sp, KV_RANK),
                                  kr[:n_p].reshape(bp, sp, C_ROPE))),
                          (s_st, (c_s, nn_s, m_s, buf_s, ckv[n_p:].reshape(bs, ts, KV_RANK),
                                  kr[n_p:].reshape(bs, ts, C_ROPE)))):
            for i in range(6):
                lst[i].append(vals[i])

    y = final_norm(x, delta, final_g)
    y_prompt = y[:n_p].reshape(bp, sp, D_MODEL)
    y_sample = y[n_p:].reshape(bs, ts, D_MODEL)
    return (y_prompt, y_sample, *[jnp.stack(a) for a in p_st], *[jnp.stack(a) for a in s_st])
```

```python
import functools

import numpy as np
import jax
import jax.numpy as jnp
from jax import lax
from jax.experimental import pallas as pl
from jax.experimental.pallas import tpu as pltpu

F32 = jnp.float32
BF16 = jnp.bfloat16
HIGHEST = lax.Precision.HIGHEST

D_MODEL = 2048
DEPTH = 4
CHUNK = 64
EPS = 1e-6
A_HEADS = 4
A_HEAD_DIM = 128
D_A = A_HEADS * A_HEAD_DIM
D_B = 512
CONV_WIDTH = 31
C_HEADS = 8
C_NOPE = 128
C_ROPE = 64
C_VDIM = 128
Q_RANK = 512
KV_RANK = 256
D_C = C_HEADS * C_VDIM
ROPE_BASE = 10000.0
ATTN_SCALE = (C_NOPE + C_ROPE) ** -0.5
Q_SCALE = ATTN_SCALE * float(np.log2(np.e))
SPLIT_SIZES = (D_A, D_A, D_A, D_A, A_HEADS, A_HEADS, 2 * D_B, Q_RANK, KV_RANK, C_ROPE)
SPLIT_IDX = tuple(int(i) for i in np.cumsum(SPLIT_SIZES)[:-1])
IN_COLS = int(sum(SPLIT_SIZES))
PEER_HEADS = 8
N_KEYS = 128
N_EXPERTS = N_KEYS * N_KEYS
D_KEY = 256
PEER_TOPK = 16

PROJ_COLS = 4096
GATE_LANE = C_ROPE
CB_Q, CB_K, CB_V, CB_O = 0, 1, 2, 3
CB_GLU = 2
CB_CQ = 6
CB_CKV = 14
CB_KRG = 30

V7X_VMEM_LIMIT = 56 * 1024 * 1024
NEG_INF = float("-inf")


def _cparams(sem):
    return pltpu.CompilerParams(dimension_semantics=sem, vmem_limit_bytes=V7X_VMEM_LIMIT)


def _rms(x, g):
    return x * lax.rsqrt(jnp.mean(x * x, axis=-1, keepdims=True) + EPS) * g


def _log_sigmoid(x):
    return jnp.minimum(x, 0.0) - jnp.log1p(jnp.exp(-jnp.abs(x)))


def _norm_matmul_kernel(*refs, has_delta, emit_xn):
    it = iter(refs)
    x_ref = next(it)
    d_ref = next(it) if has_delta else None
    g_ref = next(it)
    w_ref = next(it)
    o_ref = next(it)
    xs_ref = next(it) if has_delta else None
    xn_out_ref = next(it) if emit_xn else None
    xn_ref = next(it)

    @pl.when(pl.program_id(1) == 0)
    def _():
        x = x_ref[...]
        if has_delta:
            x = x + d_ref[...]
            xs_ref[...] = x
        yb = _rms(x, g_ref[...]).astype(BF16)
        xn_ref[...] = yb
        if emit_xn:
            xn_out_ref[...] = yb.T

    o_ref[...] = jnp.dot(xn_ref[...], w_ref[...], preferred_element_type=F32)


def norm_matmul(x, delta, g, w, *, emit_xn=False, tm=512, tn=512):
    n, d = x.shape
    m = w.shape[1]
    has_delta = delta is not None
    row_spec = pl.BlockSpec((tm, d), lambda i, j: (i, 0))
    in_specs = [row_spec] + ([row_spec] if has_delta else []) + [
        pl.BlockSpec((1, d), lambda i, j: (0, 0)),
        pl.BlockSpec((d, tn), lambda i, j: (0, j)),
    ]
    out_shape = [jax.ShapeDtypeStruct((n, m), F32)]
    out_specs = [pl.BlockSpec((tm, tn), lambda i, j: (i, j))]
    if has_delta:
        out_shape.append(jax.ShapeDtypeStruct((n, d), F32))
        out_specs.append(row_spec)
    if emit_xn:
        out_shape.append(jax.ShapeDtypeStruct((d, n), BF16))
        out_specs.append(pl.BlockSpec((d, tm), lambda i, j: (0, i)))
    args = [x] + ([delta] if has_delta else []) + [g.reshape(1, d), w]
    outs = pl.pallas_call(
        functools.partial(_norm_matmul_kernel, has_delta=has_delta, emit_xn=emit_xn),
        grid=(n // tm, m // tn),
        in_specs=in_specs,
        out_specs=out_specs,
        out_shape=out_shape,
        scratch_shapes=[pltpu.VMEM((tm, d), BF16)],
        compiler_params=_cparams(("parallel", "arbitrary")),
        name="norm_matmul",
    )(*args)
    outs = list(outs)
    proj = outs.pop(0)
    xs = outs.pop(0) if has_delta else x
    xn = outs.pop(0) if emit_xn else None
    return proj, xs, xn


def _mlstm_kernel(q_ref, k_ref, v_ref, o_ref, g_ref, c0_ref, n0_ref, m0_ref, bias_ref, ng_ref,
                  y_ref, c_ref, n_ref, m_ref):
    L = q_ref.shape[0]

    @pl.when(pl.program_id(1) == 0)
    def _():
        c_ref[...] = c0_ref[...]
        n_ref[...] = n0_ref[...]
        m_ref[...] = m0_ref[...]

    pre = g_ref[...] + bias_ref[...]
    lf_all = _log_sigmoid(pre)
    row = lax.broadcasted_iota(jnp.int32, (L, L), 0)
    col = lax.broadcasted_iota(jnp.int32, (L, L), 1)
    causal = col <= row
    tril = causal.astype(F32)
    triu = (row <= col).astype(F32)
    b_all = jnp.dot(tril, lf_all, precision=HIGHEST, preferred_element_type=F32)
    sel = (lax.broadcasted_iota(jnp.int32, (8, 128), 1)
           == lax.broadcasted_iota(jnp.int32, (8, 128), 0) + GATE_LANE).astype(F32)
    rows = lax.dot_general(sel, pre, (((1,), (1,)), ((), ())), precision=HIGHEST,
                           preferred_element_type=F32)
    ig_rows = rows[0:A_HEADS]
    lf_rows = _log_sigmoid(rows[A_HEADS:2 * A_HEADS])
    b_rows = jnp.dot(lf_rows, triu, precision=HIGHEST, preferred_element_type=F32)

    for h in range(A_HEADS):
        sl = slice(h * A_HEAD_DIM, (h + 1) * A_HEAD_DIM)
        q = q_ref[:, sl]
        k = k_ref[:, sl] * (A_HEAD_DIM ** -0.5)
        v = v_ref[:, sl]
        qb, kb, vb = q.astype(BF16), k.astype(BF16), v.astype(BF16)
        b_col = b_all[:, GATE_LANE + A_HEADS + h:GATE_LANE + A_HEADS + h + 1]
        ig_col = pre[:, GATE_LANE + h:GATE_LANE + h + 1]
        b_row = b_rows[h:h + 1]
        ig_row = ig_rows[h:h + 1]
        m_prev = m_ref[0, h:h + 1, 0:1]
        c_prev = c_ref[0, h]
        n_prev = n_ref[0, h:h + 1, :]

        dm = jnp.where(causal, b_col - b_row + ig_row, NEG_INF)
        inter = b_col + m_prev
        m_t = jnp.maximum(inter, jnp.max(dm, axis=1, keepdims=True))
        w_intra = jnp.exp(dm - m_t)
        w_inter = jnp.exp(inter - m_t)
        s = lax.dot_general(qb, kb, (((1,), (1,)), ((), ())), preferred_element_type=F32) * w_intra
        num = (jnp.dot(s.astype(BF16), vb, preferred_element_type=F32)
               + w_inter * jnp.dot(qb, c_prev.astype(BF16), preferred_element_type=F32))
        den = jnp.sum(s, axis=1, keepdims=True) + w_inter * jnp.sum(q * n_prev, axis=1, keepdims=True)
        hh = num / jnp.maximum(jnp.abs(den), jnp.exp(-m_t))

        m_new = m_t[L - 1:L, :]
        b_last = b_col[L - 1:L, :]
        decay = jnp.exp(b_last + m_prev - m_new)
        wk = jnp.exp(b_last - b_col + ig_col - m_new)
        kw = k * wk
        c_ref[0, h] = decay * c_prev + lax.dot_general(
            kw.astype(BF16), vb, (((0,), (0,)), ((), ())), preferred_element_type=F32)
        n_ref[0, h:h + 1, :] = decay * n_prev + jnp.sum(kw, axis=0, keepdims=True)
        m_ref[0, h:h + 1, :] = jnp.broadcast_to(m_new, (1, 128))

        hn = _rms(hh, ng_ref[:, sl])
        y_ref[:, sl] = (jax.nn.sigmoid(o_ref[:, sl]) * hn).astype(BF16)


def mlstm_group(proj, row0, bsz, t, chunk_len, c0, n0, m0, gate_bias, norm_g):
    nc = t // chunk_len
    rb0 = row0 // chunk_len

    def col_spec(width, cb):
        return pl.BlockSpec((chunk_len, width), lambda b, c: (rb0 + b * nc + c, cb))

    st4 = lambda b, c: (b, 0, 0, 0)
    st3 = lambda b, c: (b, 0, 0)
    m0p = jnp.pad(jnp.broadcast_to(m0[:, :, None], (bsz, A_HEADS, 128)), ((0, 0), (0, 8 - A_HEADS), (0, 0)))
    y, c_f, n_f, m_f = pl.pallas_call(
        _mlstm_kernel,
        grid=(bsz, nc),
        in_specs=[
            col_spec(D_A, CB_Q), col_spec(D_A, CB_K), col_spec(D_A, CB_V), col_spec(D_A, CB_O),
            col_spec(128, CB_KRG),
            pl.BlockSpec((1, A_HEADS, A_HEAD_DIM, A_HEAD_DIM), st4),
            pl.BlockSpec((1, A_HEADS, A_HEAD_DIM), st3),
            pl.BlockSpec((1, 8, 128), st3),
            pl.BlockSpec((1, 128), lambda b, c: (0, 0)),
            pl.BlockSpec((1, D_A), lambda b, c: (0, 0)),
        ],
        out_specs=[
            pl.BlockSpec((chunk_len, D_A), lambda b, c: (b * nc + c, 0)),
            pl.BlockSpec((1, A_HEADS, A_HEAD_DIM, A_HEAD_DIM), st4),
            pl.BlockSpec((1, A_HEADS, A_HEAD_DIM), st3),
            pl.BlockSpec((1, 8, 128), st3),
        ],
        out_shape=[
            jax.ShapeDtypeStruct((bsz * t, D_A), BF16),
            jax.ShapeDtypeStruct((bsz, A_HEADS, A_HEAD_DIM, A_HEAD_DIM), F32),
            jax.ShapeDtypeStruct((bsz, A_HEADS, A_HEAD_DIM), F32),
            jax.ShapeDtypeStruct((bsz, 8, 128), F32),
        ],
        compiler_params=_cparams(("parallel", "arbitrary")),
        name="mlstm",
    )(proj, proj, proj, proj, proj, c0, n0, m0p, gate_bias, norm_g.reshape(1, D_A))
    return y, c_f, n_f, m_f[:, :A_HEADS, 0]


CONV_HALO = 32
CONV_ROWS = 32


def _conv_kernel(glu_ref, buf_ref, w_ref, b_ref, lng_ref, lnb_ref, y_ref, nbuf_ref, xp_ref):
    tc = glu_ref.shape[0]
    j = pl.program_id(1)
    hist = CONV_WIDTH - 1

    @pl.when(j == 0)
    def _():
        xp_ref[0:CONV_HALO - hist, :] = jnp.zeros((CONV_HALO - hist, D_B), F32)
        xp_ref[CONV_HALO - hist:CONV_HALO, :] = buf_ref[0]

    @pl.when(j > 0)
    def _():
        xp_ref[0:CONV_HALO, :] = xp_ref[tc:tc + CONV_HALO, :]

    u = glu_ref[:, 0:D_B] * jax.nn.sigmoid(glu_ref[:, D_B:2 * D_B])
    xp_ref[CONV_HALO:CONV_HALO + tc, :] = u

    @pl.when(j == pl.num_programs(1) - 1)
    def _():
        nbuf_ref[0] = xp_ref[CONV_HALO + tc - hist:CONV_HALO + tc, :]

    for r0 in range(0, tc, CONV_ROWS):
        acc = jnp.broadcast_to(b_ref[...], (CONV_ROWS, D_B))
        for tap in range(CONV_WIDTH):
            start = CONV_HALO - hist + r0 + tap
            acc = acc + w_ref[tap:tap + 1, :] * xp_ref[start:start + CONV_ROWS, :]
        mu = jnp.mean(acc, axis=-1, keepdims=True)
        cen = acc - mu
        var = jnp.mean(cen * cen, axis=-1, keepdims=True)
        z = cen * lax.rsqrt(var + EPS) * lng_ref[...] + lnb_ref[...]
        y_ref[r0:r0 + CONV_ROWS, :] = (z * jax.nn.sigmoid(z)).astype(BF16)


def conv_group(proj, row0, bsz, t, tc, buf, w, b, ln_g, ln_b):
    nt = t // tc
    rb0 = row0 // tc
    const2 = lambda bb, j: (0, 0)
    y, nbuf = pl.pallas_call(
        _conv_kernel,
        grid=(bsz, nt),
        in_specs=[
            pl.BlockSpec((tc, 2 * D_B), lambda bb, j: (rb0 + bb * nt + j, CB_GLU)),
            pl.BlockSpec((1, CONV_WIDTH - 1, D_B), lambda bb, j: (bb, 0, 0)),
            pl.BlockSpec((CONV_WIDTH, D_B), const2),
            pl.BlockSpec((1, D_B), const2),
            pl.BlockSpec((1, D_B), const2),
            pl.BlockSpec((1, D_B), const2),
        ],
        out_specs=[
            pl.BlockSpec((tc, D_B), lambda bb, j: (bb * nt + j, 0)),
            pl.BlockSpec((1, CONV_WIDTH - 1, D_B), lambda bb, j: (bb, 0, 0)),
        ],
        out_shape=[
            jax.ShapeDtypeStruct((bsz * t, D_B), BF16),
            jax.ShapeDtypeStruct((bsz, CONV_WIDTH - 1, D_B), F32),
        ],
        scratch_shapes=[pltpu.VMEM((CONV_HALO + tc + CONV_HALO, D_B), F32)],
        compiler_params=_cparams(("parallel", "arbitrary")),
        name="conv_module",
    )(proj, buf, w, b.reshape(1, D_B), ln_g.reshape(1, D_B), ln_b.reshape(1, D_B))
    return y, nbuf


def _mla_prep_kernel(cq_ref, ckv_ref, krg_ref, cos_ref, sin_ref, qg_ref, kvg_ref, wuq_ref, wuk_ref,
                     qlat_ref, qrope_ref, ckv_out_ref, kr_out_ref, ckv_b_ref, kr_b_ref):
    half = C_ROPE // 2
    cq = _rms(cq_ref[...], qg_ref[...]).astype(BF16)
    qh = jnp.dot(cq, wuq_ref[...], preferred_element_type=F32)
    cos = cos_ref[...]
    sin = sin_ref[...]
    x1 = qh[:, D_C:D_C + C_HEADS * half]
    x2 = qh[:, D_C + C_HEADS * half:D_C + 2 * C_HEADS * half]
    r1 = (x1 * cos - x2 * sin) * Q_SCALE
    r2 = (x2 * cos + x1 * sin) * Q_SCALE
    for h in range(C_HEADS):
        qn = qh[:, h * C_NOPE:(h + 1) * C_NOPE].astype(BF16)
        ql = lax.dot_general(qn, wuk_ref[h], (((1,), (1,)), ((), ())), preferred_element_type=F32)
        qlat_ref[h] = (ql * Q_SCALE).astype(BF16)
        qrope_ref[h] = jnp.concatenate(
            [r1[:, h * half:(h + 1) * half], r2[:, h * half:(h + 1) * half]], axis=1).astype(BF16)
    ckv = _rms(ckv_ref[...], kvg_ref[...])
    ckv_out_ref[...] = ckv
    ckv_b_ref[...] = ckv.astype(BF16)
    k1 = krg_ref[:, 0:half]
    k2 = krg_ref[:, half:C_ROPE]
    c1 = cos[:, 0:half]
    s1 = sin[:, 0:half]
    kr = jnp.concatenate([k1 * c1 - k2 * s1, k2 * c1 + k1 * s1], axis=1)
    kr_out_ref[...] = kr
    kr_b_ref[...] = kr.astype(BF16)


def mla_prep(proj, cos_t, sin_t, q_norm_g, kv_norm_g, w_uq_r, w_uk_b, *, tq=256):
    n = proj.shape[0]
    const2 = lambda i: (0, 0)
    return pl.pallas_call(
        _mla_prep_kernel,
        grid=(n // tq,),
        in_specs=[
            pl.BlockSpec((tq, Q_RANK), lambda i: (i, CB_CQ)),
            pl.BlockSpec((tq, KV_RANK), lambda i: (i, CB_CKV)),
            pl.BlockSpec((tq, 128), lambda i: (i, CB_KRG)),
            pl.BlockSpec((tq, C_HEADS * C_ROPE // 2), lambda i: (i, 0)),
            pl.BlockSpec((tq, C_HEADS * C_ROPE // 2), lambda i: (i, 0)),
            pl.BlockSpec((1, Q_RANK), const2),
            pl.BlockSpec((1, KV_RANK), const2),
            pl.BlockSpec((Q_RANK, C_HEADS * (C_NOPE + C_ROPE)), const2),
            pl.BlockSpec((C_HEADS, KV_RANK, C_NOPE), lambda i: (0, 0, 0)),
        ],
        out_specs=[
            pl.BlockSpec((C_HEADS, tq, KV_RANK), lambda i: (0, i, 0)),
            pl.BlockSpec((C_HEADS, tq, C_ROPE), lambda i: (0, i, 0)),
            pl.BlockSpec((tq, KV_RANK), lambda i: (i, 0)),
            pl.BlockSpec((tq, C_ROPE), lambda i: (i, 0)),
            pl.BlockSpec((tq, KV_RANK), lambda i: (i, 0)),
            pl.BlockSpec((tq, C_ROPE), lambda i: (i, 0)),
        ],
        out_shape=[
            jax.ShapeDtypeStruct((C_HEADS, n, KV_RANK), BF16),
            jax.ShapeDtypeStruct((C_HEADS, n, C_ROPE), BF16),
            jax.ShapeDtypeStruct((n, KV_RANK), F32),
            jax.ShapeDtypeStruct((n, C_ROPE), F32),
            jax.ShapeDtypeStruct((n, KV_RANK), BF16),
            jax.ShapeDtypeStruct((n, C_ROPE), BF16),
        ],
        compiler_params=_cparams(("parallel",)),
        name="mla_prep",
    )(proj, proj, proj, cos_t, sin_t, q_norm_g.reshape(1, Q_RANK), kv_norm_g.reshape(1, KV_RANK), w_uq_r, w_uk_b)


ATT_SLAB = 256
LANES = 128
_NT = (((1,), (1,)), ((), ()))


def _attn_slab(q, qr, kcb, krb, bias, rows, m_ref, l_ref, acc_ref):
    s = (lax.dot_general(q, kcb, _NT, preferred_element_type=F32)
         + lax.dot_general(qr, krb, _NT, preferred_element_type=F32))
    if bias is not None:
        s = s + bias
    tiles = [s[:, t * LANES:(t + 1) * LANES] for t in range(s.shape[1] // LANES)]
    mx = functools.reduce(jnp.maximum, tiles)
    m_prev = m_ref[rows, :]
    m_new = jnp.maximum(m_prev, jnp.max(mx, axis=1, keepdims=True))
    alpha = jnp.exp2(m_prev - m_new)
    ps = [jnp.exp2(t - m_new) for t in tiles]
    l_ref[rows, :] = alpha * l_ref[rows, :] + functools.reduce(jnp.add, ps)
    p = jnp.concatenate(ps, axis=1).astype(BF16)
    pv = jnp.dot(p, kcb, preferred_element_type=F32)
    acc_ref[rows, :] = jnp.concatenate([alpha] * (KV_RANK // LANES), axis=1) * acc_ref[rows, :] + pv
    m_ref[rows, :] = m_new


def _attn_finish(acc_ref, l_ref, wuv_ref, y_ref, tq):
    for h in range(C_HEADS):
        rows = slice(h * tq, (h + 1) * tq)
        o = acc_ref[rows, :] / jnp.sum(l_ref[rows, :], axis=1, keepdims=True)
        y_ref[:, h * C_VDIM:(h + 1) * C_VDIM] = jnp.dot(
            o.astype(BF16), wuv_ref[h], preferred_element_type=F32).astype(BF16)


def _attn_init(m_ref, l_ref, acc_ref):
    m_ref[...] = jnp.full(m_ref.shape, NEG_INF, F32)
    l_ref[...] = jnp.zeros(l_ref.shape, F32)
    acc_ref[...] = jnp.zeros(acc_ref.shape, F32)


def _attn_prompt_kernel(ql_ref, qr_ref, k_ref, kr_ref, wuv_ref, y_ref, m_ref, l_ref, acc_ref, *, kb):
    tq = ql_ref.shape[1]
    i = pl.program_id(1)
    _attn_init(m_ref, l_ref, acc_ref)

    def block(start, bias):
        kcb = k_ref[pl.ds(start, kb), :]
        krb = kr_ref[pl.ds(start, kb), :]
        for h in range(C_HEADS):
            _attn_slab(ql_ref[h], qr_ref[h], kcb, krb, bias, slice(h * tq, (h + 1) * tq), m_ref, l_ref, acc_ref)

    diag = (i * tq) // kb

    def body(j, carry):
        block(pl.multiple_of(j * kb, kb), None)
        return carry

    lax.fori_loop(0, diag, body, 0)
    start = pl.multiple_of(diag * kb, kb)
    qchunk = (i * tq + lax.broadcasted_iota(jnp.int32, (tq, kb), 0)) // CHUNK
    kchunk = (start + lax.broadcasted_iota(jnp.int32, (tq, kb), 1)) // CHUNK
    block(start, jnp.where(kchunk <= qchunk, 0.0, NEG_INF))
    _attn_finish(acc_ref, l_ref, wuv_ref, y_ref, tq)


def attn_prompt(q_lat, q_rope, ckv_b, kr_b, w_uv_b, bsz, t, *, tq=ATT_SLAB, kb=512):
    nq = t // tq
    rows = C_HEADS * tq
    return pl.pallas_call(
        functools.partial(_attn_prompt_kernel, kb=kb),
        grid=(bsz, nq),
        in_specs=[
            pl.BlockSpec((C_HEADS, tq, KV_RANK), lambda b, i: (0, b * nq + i, 0)),
            pl.BlockSpec((C_HEADS, tq, C_ROPE), lambda b, i: (0, b * nq + i, 0)),
            pl.BlockSpec((t, KV_RANK), lambda b, i: (b, 0)),
            pl.BlockSpec((t, C_ROPE), lambda b, i: (b, 0)),
            pl.BlockSpec((C_HEADS, KV_RANK, C_VDIM), lambda b, i: (0, 0, 0)),
        ],
        out_specs=pl.BlockSpec((tq, D_C), lambda b, i: (b * nq + i, 0)),
        out_shape=jax.ShapeDtypeStruct((bsz * t, D_C), BF16),
        scratch_shapes=[pltpu.VMEM((rows, LANES), F32), pltpu.VMEM((rows, LANES), F32),
                        pltpu.VMEM((rows, KV_RANK), F32)],
        compiler_params=_cparams(("parallel", "arbitrary")),
        name="attn_prompt",
    )(q_lat, q_rope, ckv_b, kr_b, w_uv_b)


def _attn_sample_kernel(ql_ref, qr_ref, k_ref, kr_ref, ck_ref, ckr_ref, wuv_ref, y_ref, m_ref, l_ref, acc_ref,
                        *, kb):
    tq = ql_ref.shape[1]
    rows_total = C_HEADS * tq
    hps = ATT_SLAB // tq
    _attn_init(m_ref, l_ref, acc_ref)
    past = ck_ref.shape[1]

    def block(kcb, krb, bias):
        for sidx in range(rows_total // ATT_SLAB):
            q = ql_ref[sidx * hps:(sidx + 1) * hps].reshape(ATT_SLAB, KV_RANK)
            qr = qr_ref[sidx * hps:(sidx + 1) * hps].reshape(ATT_SLAB, C_ROPE)
            _attn_slab(q, qr, kcb, krb, bias, slice(sidx * ATT_SLAB, (sidx + 1) * ATT_SLAB), m_ref, l_ref, acc_ref)

    def body(j, carry):
        start = pl.multiple_of(j * kb, kb)
        block(ck_ref[0, pl.ds(start, kb), :].astype(BF16), ckr_ref[0, pl.ds(start, kb), :].astype(BF16), None)
        return carry

    lax.fori_loop(0, past // kb, body, 0)
    pad = LANES - tq
    kcb = jnp.concatenate([k_ref[...], jnp.zeros((pad, KV_RANK), BF16)], axis=0)
    krb = jnp.concatenate([kr_ref[...], jnp.zeros((pad, C_ROPE), BF16)], axis=0)
    bias = jnp.where(lax.broadcasted_iota(jnp.int32, (1, LANES), 1) < tq, 0.0, NEG_INF)
    block(kcb, krb, bias)
    _attn_finish(acc_ref, l_ref, wuv_ref, y_ref, tq)


def attn_sample(q_lat, q_rope, ckv_b, kr_b, cache_lat, cache_rope, w_uv_b, row0, bsz, t, *, kb=1024):
    rb0 = row0 // t
    past = cache_lat.shape[1]
    rows = C_HEADS * t
    return pl.pallas_call(
        functools.partial(_attn_sample_kernel, kb=kb),
        grid=(bsz,),
        in_specs=[
            pl.BlockSpec((C_HEADS, t, KV_RANK), lambda b: (0, rb0 + b, 0)),
            pl.BlockSpec((C_HEADS, t, C_ROPE), lambda b: (0, rb0 + b, 0)),
            pl.BlockSpec((t, KV_RANK), lambda b: (rb0 + b, 0)),
            pl.BlockSpec((t, C_ROPE), lambda b: (rb0 + b, 0)),
            pl.BlockSpec((1, past, KV_RANK), lambda b: (b, 0, 0)),
            pl.BlockSpec((1, past, C_ROPE), lambda b: (b, 0, 0)),
            pl.BlockSpec((C_HEADS, KV_RANK, C_VDIM), lambda b: (0, 0, 0)),
        ],
        out_specs=pl.BlockSpec((t, D_C), lambda b: (b, 0)),
        out_shape=jax.ShapeDtypeStruct((bsz * t, D_C), BF16),
        scratch_shapes=[pltpu.VMEM((rows, LANES), F32), pltpu.VMEM((rows, LANES), F32),
                        pltpu.VMEM((rows, KV_RANK), F32)],
        compiler_params=_cparams(("parallel",)),
        name="attn_sample",
    )(q_lat, q_rope, ckv_b, kr_b, cache_lat, cache_rope, w_uv_b)


def _oproj_kernel(x_ref, ya_ref, yb_ref, yc_ref, wa_ref, wb_ref, wc_ref, o_ref):
    acc = x_ref[...]
    acc = acc + jnp.dot(ya_ref[...], wa_ref[...], preferred_element_type=F32)
    acc = acc + jnp.dot(yb_ref[...], wb_ref[...], preferred_element_type=F32)
    acc = acc + jnp.dot(yc_ref[...], wc_ref[...], preferred_element_type=F32)
    o_ref[...] = acc


def out_proj(x, ya, yb, yc, w_out_b, *, tm=1024, tn=512):
    n, d = x.shape
    return pl.pallas_call(
        _oproj_kernel,
        grid=(n // tm, d // tn),
        in_specs=[
            pl.BlockSpec((tm, tn), lambda i, j: (i, j)),
            pl.BlockSpec((tm, D_A), lambda i, j: (i, 0)),
            pl.BlockSpec((tm, D_B), lambda i, j: (i, 0)),
            pl.BlockSpec((tm, D_C), lambda i, j: (i, 0)),
            pl.BlockSpec((D_A, tn), lambda i, j: (0, j)),
            pl.BlockSpec((D_B, tn), lambda i, j: (1, j)),
            pl.BlockSpec((D_C, tn), lambda i, j: (1, j)),
        ],
        out_specs=pl.BlockSpec((tm, tn), lambda i, j: (i, j)),
        out_shape=jax.ShapeDtypeStruct((n, d), F32),
        compiler_params=_cparams(("parallel", "arbitrary")),
        name="out_proj",
    )(x, ya, yb, yc, w_out_b, w_out_b, w_out_b)


STEP_SCALE = 2.0 ** 64


def _extract_max(cur):
    m = jnp.max(cur, axis=0, keepdims=True)
    return m, jnp.where(cur == m, NEG_INF, cur)


def _peer_select_kernel(q_ref, keys_ref, s2_ref, e2_ref, thr_ref, e1_ref):
    for h in range(PEER_HEADS):
        st = []
        top = []
        for z in range(2):
            qhz = q_ref[:, (2 * h + z) * N_KEYS:(2 * h + z + 1) * N_KEYS].astype(BF16)
            s = lax.dot_general(keys_ref[h, z], qhz, (((1,), (1,)), ((), ())),
                                preferred_element_type=F32)
            st.append(s)
            cur = s
            vals = []
            for _ in range(PEER_TOPK):
                m, cur = _extract_max(cur)
                vals.append(m)
            top.append(vals)
        v2 = jnp.concatenate(top[1], axis=0)
        cand = jnp.concatenate([top[0][a] + v2 for a in range(PEER_TOPK)], axis=0)
        cur = cand
        best, cur = _extract_max(cur)
        zsum = jnp.ones_like(best)
        m = best
        for _ in range(PEER_TOPK - 1):
            m, cur = _extract_max(cur)
            zsum = zsum + jnp.exp(m - best)
        nxt = jnp.max(cur, axis=0, keepdims=True)
        tau = 0.5 * (m + nxt)
        s2_ref[h] = st[1] * STEP_SCALE
        e2_ref[h] = jnp.exp(st[1] - top[1][0]) / zsum
        thr_ref[h] = (tau - st[0]) * STEP_SCALE
        e1_ref[h] = jnp.exp(st[0] - top[0][0])


def peer_select(q, keys_b, *, tp=256):
    n = q.shape[0]
    shp = jax.ShapeDtypeStruct((PEER_HEADS, N_KEYS, n), F32)
    spec = pl.BlockSpec((PEER_HEADS, N_KEYS, tp), lambda i: (0, 0, i))
    return pl.pallas_call(
        _peer_select_kernel,
        grid=(n // tp,),
        in_specs=[
            pl.BlockSpec((tp, PEER_HEADS * D_KEY), lambda i: (i, 0)),
            pl.BlockSpec((PEER_HEADS, 2, N_KEYS, D_KEY // 2), lambda i: (0, 0, 0, 0)),
        ],
        out_specs=[spec, spec, spec, spec],
        out_shape=[shp, shp, shp, shp],
        compiler_params=_cparams(("parallel",)),
        name="peer_select",
    )(q, keys_b)


PEER_CHUNK = 256


def _peer_dense_kernel(xnt_ref, s2_ref, e2_ref, thr_ref, e1_ref, u_ref, vt_ref, o_ref, wt_ref, acc_ref, *, eb):
    j = pl.program_id(1)
    nchunk = eb // PEER_CHUNK
    per = PEER_CHUNK // N_KEYS

    @pl.when(j == 0)
    def _():
        acc_ref[...] = jnp.zeros(acc_ref.shape, F32)

    def scores(c):
        return jnp.dot(u_ref[c * PEER_CHUNK:(c + 1) * PEER_CHUNK, :], xnt_ref[...], preferred_element_type=F32)

    def gate_chunk(c, at):
        for ii in range(per):
            row = c * per + ii
            thr_rows = [thr_ref[h, row:row + 1, :] for h in range(PEER_HEADS)]
            e1_rows = [e1_ref[h, row:row + 1, :] for h in range(PEER_HEADS)]
            for t in range(xnt_ref.shape[1] // LANES):
                cols = slice(t * LANES, (t + 1) * LANES)
                g = None
                for h in range(PEER_HEADS):
                    term = (jnp.maximum(jnp.minimum(s2_ref[h, :, cols] - thr_rows[h][:, cols],
                                                    e2_ref[h, :, cols]), 0.0) * e1_rows[h][:, cols])
                    g = term if g is None else g + term
                a = at[ii * N_KEYS:(ii + 1) * N_KEYS, cols]
                act = 0.5 * a * (1.0 + lax.erf(a * (2.0 ** -0.5)))
                wt_ref[row * N_KEYS:(row + 1) * N_KEYS, cols] = (g * act).astype(BF16)

    at_next = scores(0)
    for c in range(nchunk):
        at = at_next
        if c + 1 < nchunk:
            at_next = scores(c + 1)
        gate_chunk(c, at)
        rows = slice(c * PEER_CHUNK, (c + 1) * PEER_CHUNK)
        acc_ref[...] += jnp.dot(vt_ref[:, rows], wt_ref[rows, :], preferred_element_type=F32)

    @pl.when(j == pl.num_programs(1) - 1)
    def _():
        o_ref[...] = acc_ref[...].T


def peer_dense(xnt, s2, e2, thr, e1, u_b, vt_b, *, tp=512, eb=1024):
    d, n = xnt.shape
    nb = N_EXPERTS // eb
    once = pl.Buffered(1)
    gspec = pl.BlockSpec((PEER_HEADS, N_KEYS, tp), lambda i, j: (0, 0, i), pipeline_mode=once)
    rspec = pl.BlockSpec((PEER_HEADS, eb // N_KEYS, tp), lambda i, j: (0, j, i))
    return pl.pallas_call(
        functools.partial(_peer_dense_kernel, eb=eb),
        grid=(n // tp, nb),
        in_specs=[
            pl.BlockSpec((d, tp), lambda i, j: (0, i), pipeline_mode=once),
            gspec, gspec, rspec, rspec,
            pl.BlockSpec((eb, d), lambda i, j: (j, 0)),
            pl.BlockSpec((d, eb), lambda i, j: (0, j)),
        ],
        out_specs=pl.BlockSpec((tp, d), lambda i, j: (i, 0)),
        out_shape=jax.ShapeDtypeStruct((n, d), F32),
        scratch_shapes=[pltpu.VMEM((eb, tp), BF16), pltpu.VMEM((d, tp), F32)],
        compiler_params=_cparams(("parallel", "arbitrary")),
        name="peer_dense",
    )(xnt, s2, e2, thr, e1, u_b, vt_b)


def _final_norm_kernel(x_ref, d_ref, g_ref, o_ref):
    o_ref[...] = _rms(x_ref[...] + d_ref[...], g_ref[...])


def final_norm(x, delta, g, *, tm=512):
    n, d = x.shape
    spec = pl.BlockSpec((tm, d), lambda i: (i, 0))
    return pl.pallas_call(
        _final_norm_kernel,
        grid=(n // tm,),
        in_specs=[spec, spec, pl.BlockSpec((1, d), lambda i: (0, 0))],
        out_specs=spec,
        out_shape=jax.ShapeDtypeStruct((n, d), F32),
        compiler_params=_cparams(("parallel",)),
        name="final_norm",
    )(x, delta, g.reshape(1, d))


def _prep_w_in(w_in):
    qa, ka, va, oa, ig, fg, glu, cq, ckv, kr = jnp.split(w_in, SPLIT_IDX, axis=-1)
    pad = jnp.zeros(w_in.shape[:-1] + (PROJ_COLS - IN_COLS,), w_in.dtype)
    return jnp.concatenate([qa, ka, va, oa, glu, cq, ckv, kr, ig, fg, pad], axis=-1).astype(BF16)


def _prep_w_uq(w_uq):
    w = w_uq.reshape(w_uq.shape[:-1] + (C_HEADS, C_NOPE + C_ROPE))
    half = C_ROPE // 2
    flat = lambda a: a.reshape(a.shape[:-2] + (-1,))
    return jnp.concatenate(
        [flat(w[..., :C_NOPE]), flat(w[..., C_NOPE:C_NOPE + half]), flat(w[..., C_NOPE + half:])],
        axis=-1).astype(BF16)


def _rope_tables(pos):
    half = C_ROPE // 2
    inv = ROPE_BASE ** (-jnp.arange(half, dtype=jnp.float32) / half)
    ang = pos.astype(jnp.float32)[:, None] * inv[None, :]
    return jnp.tile(jnp.cos(ang), (1, C_HEADS)), jnp.tile(jnp.sin(ang), (1, C_HEADS))


def kernel(x_prompt, x_sample, state_mlstm_C, state_mlstm_n, state_mlstm_m, state_conv, cache_kv_latent, cache_k_rope, norm1_g, w_in, b_gate, mlstm_norm_g, conv_w, conv_b, conv_ln_g, conv_ln_b, q_norm_g, kv_norm_g, w_uq, w_uk, w_uv, w_out, norm2_g, peer_wq, peer_keys, peer_u, peer_v, final_g):
    bp, sp, _ = x_prompt.shape
    bs, ts, _ = x_sample.shape
    past = cache_kv_latent.shape[2]
    n_p = bp * sp
    n_s = bs * ts

    x = jnp.concatenate([x_prompt.reshape(n_p, D_MODEL), x_sample.reshape(n_s, D_MODEL)], axis=0)
    pos = jnp.concatenate([jnp.tile(jnp.arange(sp), bp), jnp.tile(past + jnp.arange(ts), bs)])
    cos_t, sin_t = _rope_tables(pos)

    w_in_b = _prep_w_in(w_in)
    w_uq_b = _prep_w_uq(w_uq)
    w_uk_b = w_uk.astype(BF16)
    w_uv_b = w_uv.astype(BF16)
    w_out_b = w_out.astype(BF16)
    peer_wq_b = peer_wq.astype(BF16)
    peer_keys_b = peer_keys.astype(BF16)
    peer_u_b = peer_u.astype(BF16)
    peer_vt_b = jnp.swapaxes(peer_v, 1, 2).astype(BF16)
    gate_bias = jnp.pad(b_gate, ((0, 0), (GATE_LANE, 128 - GATE_LANE - 2 * A_HEADS)))[:, None, :]

    zc = jnp.zeros((bp, A_HEADS, A_HEAD_DIM, A_HEAD_DIM), F32)
    zn = jnp.zeros((bp, A_HEADS, A_HEAD_DIM), F32)
    zm = jnp.zeros((bp, A_HEADS), F32)
    zbuf = jnp.zeros((bp, CONV_WIDTH - 1, D_B), F32)

    p_st = [[] for _ in range(6)]
    s_st = [[] for _ in range(6)]
    delta = None
    for l in range(DEPTH):
        proj, x, _ = norm_matmul(x, delta, norm1_g[l], w_in_b[l])
        ya_p, c_p, nn_p, m_p = mlstm_group(proj, 0, bp, sp, CHUNK, zc, zn, zm, gate_bias[l], mlstm_norm_g[l])
        ya_s, c_s, nn_s, m_s = mlstm_group(proj, n_p, bs, ts, ts, state_mlstm_C[l], state_mlstm_n[l],
                                           state_mlstm_m[l], gate_bias[l], mlstm_norm_g[l])
        yb_p, buf_p = conv_group(proj, 0, bp, sp, 256, zbuf, conv_w[l], conv_b[l], conv_ln_g[l], conv_ln_b[l])
        yb_s, buf_s = conv_group(proj, n_p, bs, ts, ts, state_conv[l], conv_w[l], conv_b[l], conv_ln_g[l],
                                 conv_ln_b[l])
        q_lat, q_rope, ckv, kr, ckv_b, kr_b = mla_prep(proj, cos_t, sin_t, q_norm_g[l], kv_norm_g[l], w_uq_b[l],
                                                       w_uk_b[l])
        yc_p = attn_prompt(q_lat, q_rope, ckv_b, kr_b, w_uv_b[l], bp, sp)
        yc_s = attn_sample(q_lat, q_rope, ckv_b, kr_b, cache_kv_latent[l], cache_k_rope[l], w_uv_b[l], n_p, bs, ts)
        ya = jnp.concatenate([ya_p, ya_s], axis=0)
        yb = jnp.concatenate([yb_p, yb_s], axis=0)
        yc = jnp.concatenate([yc_p, yc_s], axis=0)
        x = out_proj(x, ya, yb, yc, w_out_b[l])
        q, _, xnt = norm_matmul(x, None, norm2_g[l], peer_wq_b[l], emit_xn=True)
        s2, e2, thr, e1 = peer_select(q, peer_keys_b[l])
        delta = peer_dense(xnt, s2, e2, thr, e1, peer_u_b[l], peer_vt_b[l])

        for lst, vals in ((p_st, (c_p, nn_p, m_p, buf_p, ckv[:n_p].reshape(bp, sp, KV_RANK),
                                  kr[:n_p].reshape(bp, sp, C_ROPE))),
                          (s_st, (c_s, nn_s, m_s, buf_s, ckv[n_p:].reshape(bs, ts, KV_RANK),
                                  kr[n_p:].reshape(bs, ts, C_ROPE)))):
            for i in range(6):
                lst[i].append(vals[i])

    y = final_norm(x, delta, final_g)
    y_prompt = y[:n_p].reshape(bp, sp, D_MODEL)
    y_sample = y[n_p:].reshape(bs, ts, D_MODEL)
    return (y_prompt, y_sample, *[jnp.stack(a) for a in p_st], *[jnp.stack(a) for a in s_st])
```

```python
import functools

import numpy as np
import jax
import jax.numpy as jnp
from jax import lax
from jax.experimental import pallas as pl
from jax.experimental.pallas import tpu as pltpu

F32 = jnp.float32
BF16 = jnp.bfloat16
HIGHEST = lax.Precision.HIGHEST

D_MODEL = 2048
DEPTH = 4
CHUNK = 64
EPS = 1e-6
A_HEADS = 4
A_HEAD_DIM = 128
D_A = A_HEADS * A_HEAD_DIM
D_B = 512
CONV_WIDTH = 31
C_HEADS = 8
C_NOPE = 128
C_ROPE = 64
C_VDIM = 128
Q_RANK = 512
KV_RANK = 256
D_C = C_HEADS * C_VDIM
ROPE_BASE = 10000.0
ATTN_SCALE = (C_NOPE + C_ROPE) ** -0.5
Q_SCALE = ATTN_SCALE * float(np.log2(np.e))
SPLIT_SIZES = (D_A, D_A, D_A, D_A, A_HEADS, A_HEADS, 2 * D_B, Q_RANK, KV_RANK, C_ROPE)
SPLIT_IDX = tuple(int(i) for i in np.cumsum(SPLIT_SIZES)[:-1])
IN_COLS = int(sum(SPLIT_SIZES))
PEER_HEADS = 8
N_KEYS = 128
N_EXPERTS = N_KEYS * N_KEYS
D_KEY = 256
PEER_TOPK = 16

PROJ_COLS = 4096
GATE_LANE = C_ROPE
CB_Q, CB_K, CB_V, CB_O = 0, 1, 2, 3
CB_GLU = 2
CB_CQ = 6
CB_CKV = 14
CB_KRG = 30

V7X_VMEM_LIMIT = 60 * 1024 * 1024
NEG_INF = float("-inf")


def _cparams(sem):
    return pltpu.CompilerParams(dimension_semantics=sem, vmem_limit_bytes=V7X_VMEM_LIMIT)


def _rms(x, g):
    return x * lax.rsqrt(jnp.mean(x * x, axis=-1, keepdims=True) + EPS) * g


def _log_sigmoid(x):
    return jnp.minimum(x, 0.0) - jnp.log1p(jnp.exp(-jnp.abs(x)))


def _norm_matmul_kernel(x_ref, g_ref, w_ref, o_ref, *rest, emit_xn):
    xn_out_ref = rest[0] if emit_xn else None
    xn_ref = rest[-1]

    @pl.when(pl.program_id(1) == 0)
    def _():
        yb = _rms(x_ref[...], g_ref[...]).astype(BF16)
        xn_ref[...] = yb
        if emit_xn:
            xn_out_ref[...] = yb.T

    o_ref[...] = jnp.dot(xn_ref[...], w_ref[...], preferred_element_type=F32)


def norm_matmul(x, g, w_all, layer, *, emit_xn=False, tm=1024, tn=1024):
    n, d = x.shape
    m = w_all.shape[2]
    out_shape = [jax.ShapeDtypeStruct((n, m), F32)]
    out_specs = [pl.BlockSpec((tm, tn), lambda i, j: (i, j))]
    if emit_xn:
        out_shape.append(jax.ShapeDtypeStruct((d, n), BF16))
        out_specs.append(pl.BlockSpec((d, tm), lambda i, j: (0, i)))
    outs = pl.pallas_call(
        functools.partial(_norm_matmul_kernel, emit_xn=emit_xn),
        grid=(n // tm, m // tn),
        in_specs=[
            pl.BlockSpec((tm, d), lambda i, j: (i, 0)),
            pl.BlockSpec((1, d), lambda i, j: (0, 0)),
            pl.BlockSpec((None, d, tn), lambda i, j: (layer, 0, j)),
        ],
        out_specs=out_specs,
        out_shape=out_shape,
        scratch_shapes=[pltpu.VMEM((tm, d), BF16)],
        compiler_params=_cparams(("parallel", "arbitrary")),
        name="norm_matmul",
    )(x, g.reshape(1, d), w_all)
    return (outs[0], outs[1]) if emit_xn else (outs[0], None)


def _mlstm_kernel(q_ref, k_ref, v_ref, o_ref, g_ref, c0_ref, n0_ref, m0_ref, bias_ref, ng_ref,
                  y_ref, c_ref, n_ref, m_ref):
    L = q_ref.shape[0]

    @pl.when(pl.program_id(1) == 0)
    def _():
        c_ref[...] = c0_ref[...]
        n_ref[...] = n0_ref[...]
        m_ref[...] = m0_ref[...]

    pre = g_ref[...] + bias_ref[...]
    lf_all = _log_sigmoid(pre)
    row = lax.broadcasted_iota(jnp.int32, (L, L), 0)
    col = lax.broadcasted_iota(jnp.int32, (L, L), 1)
    causal = col <= row
    tril = causal.astype(F32)
    triu = (row <= col).astype(F32)
    b_all = jnp.dot(tril, lf_all, precision=HIGHEST, preferred_element_type=F32)
    sel = (lax.broadcasted_iota(jnp.int32, (8, 128), 1)
           == lax.broadcasted_iota(jnp.int32, (8, 128), 0) + GATE_LANE).astype(F32)
    rows = lax.dot_general(sel, pre, (((1,), (1,)), ((), ())), precision=HIGHEST,
                           preferred_element_type=F32)
    ig_rows = rows[0:A_HEADS]
    lf_rows = _log_sigmoid(rows[A_HEADS:2 * A_HEADS])
    b_rows = jnp.dot(lf_rows, triu, precision=HIGHEST, preferred_element_type=F32)

    for h in range(A_HEADS):
        sl = slice(h * A_HEAD_DIM, (h + 1) * A_HEAD_DIM)
        q = q_ref[:, sl]
        k = k_ref[:, sl] * (A_HEAD_DIM ** -0.5)
        v = v_ref[:, sl]
        qb, kb, vb = q.astype(BF16), k.astype(BF16), v.astype(BF16)
        b_col = b_all[:, GATE_LANE + A_HEADS + h:GATE_LANE + A_HEADS + h + 1]
        ig_col = pre[:, GATE_LANE + h:GATE_LANE + h + 1]
        b_row = b_rows[h:h + 1]
        ig_row = ig_rows[h:h + 1]
        m_prev = m_ref[0, h:h + 1, 0:1]
        c_prev = c_ref[0, h]
        n_prev = n_ref[0, h:h + 1, :]

        dm = jnp.where(causal, b_col - b_row + ig_row, NEG_INF)
        inter = b_col + m_prev
        m_t = jnp.maximum(inter, jnp.max(dm, axis=1, keepdims=True))
        w_intra = jnp.exp(dm - m_t)
        w_inter = jnp.exp(inter - m_t)
        s = lax.dot_general(qb, kb, (((1,), (1,)), ((), ())), preferred_element_type=F32) * w_intra
        num = (jnp.dot(s.astype(BF16), vb, preferred_element_type=F32)
               + w_inter * jnp.dot(qb, c_prev.astype(BF16), preferred_element_type=F32))
        den = jnp.sum(s, axis=1, keepdims=True) + w_inter * jnp.sum(q * n_prev, axis=1, keepdims=True)
        hh = num / jnp.maximum(jnp.abs(den), jnp.exp(-m_t))

        m_new = m_t[L - 1:L, :]
        b_last = b_col[L - 1:L, :]
        decay = jnp.exp(b_last + m_prev - m_new)
        wk = jnp.exp(b_last - b_col + ig_col - m_new)
        kw = k * wk
        c_ref[0, h] = decay * c_prev + lax.dot_general(
            kw.astype(BF16), vb, (((0,), (0,)), ((), ())), preferred_element_type=F32)
        n_ref[0, h:h + 1, :] = decay * n_prev + jnp.sum(kw, axis=0, keepdims=True)
        m_ref[0, h:h + 1, :] = jnp.broadcast_to(m_new, (1, 128))

        hn = _rms(hh, ng_ref[:, sl])
        y_ref[:, sl] = (jax.nn.sigmoid(o_ref[:, sl]) * hn).astype(BF16)


def mlstm_group(proj, row0, bsz, t, chunk_len, c0, n0, m0, gate_bias, norm_g):
    nc = t // chunk_len
    rb0 = row0 // chunk_len

    def col_spec(width, cb):
        return pl.BlockSpec((chunk_len, width), lambda b, c: (rb0 + b * nc + c, cb))

    st4 = lambda b, c: (b, 0, 0, 0)
    st3 = lambda b, c: (b, 0, 0)
    m0p = jnp.pad(jnp.broadcast_to(m0[:, :, None], (bsz, A_HEADS, 128)), ((0, 0), (0, 8 - A_HEADS), (0, 0)))
    y, c_f, n_f, m_f = pl.pallas_call(
        _mlstm_kernel,
        grid=(bsz, nc),
        in_specs=[
            col_spec(D_A, CB_Q), col_spec(D_A, CB_K), col_spec(D_A, CB_V), col_spec(D_A, CB_O),
            col_spec(128, CB_KRG),
            pl.BlockSpec((1, A_HEADS, A_HEAD_DIM, A_HEAD_DIM), st4),
            pl.BlockSpec((1, A_HEADS, A_HEAD_DIM), st3),
            pl.BlockSpec((1, 8, 128), st3),
            pl.BlockSpec((1, 128), lambda b, c: (0, 0)),
            pl.BlockSpec((1, D_A), lambda b, c: (0, 0)),
        ],
        out_specs=[
            pl.BlockSpec((chunk_len, D_A), lambda b, c: (b * nc + c, 0)),
            pl.BlockSpec((1, A_HEADS, A_HEAD_DIM, A_HEAD_DIM), st4),
            pl.BlockSpec((1, A_HEADS, A_HEAD_DIM), st3),
            pl.BlockSpec((1, 8, 128), st3),
        ],
        out_shape=[
            jax.ShapeDtypeStruct((bsz * t, D_A), BF16),
            jax.ShapeDtypeStruct((bsz, A_HEADS, A_HEAD_DIM, A_HEAD_DIM), F32),
            jax.ShapeDtypeStruct((bsz, A_HEADS, A_HEAD_DIM), F32),
            jax.ShapeDtypeStruct((bsz, 8, 128), F32),
        ],
        compiler_params=_cparams(("parallel", "arbitrary")),
        name="mlstm",
    )(proj, proj, proj, proj, proj, c0, n0, m0p, gate_bias, norm_g.reshape(1, D_A))
    return y, c_f, n_f, m_f[:, :A_HEADS, 0]


CONV_HALO = 32
CONV_ROWS = 32


def _conv_kernel(glu_ref, buf_ref, w_ref, b_ref, lng_ref, lnb_ref, y_ref, nbuf_ref, xp_ref):
    tc = glu_ref.shape[0]
    j = pl.program_id(1)
    hist = CONV_WIDTH - 1

    @pl.when(j == 0)
    def _():
        xp_ref[0:CONV_HALO - hist, :] = jnp.zeros((CONV_HALO - hist, D_B), F32)
        xp_ref[CONV_HALO - hist:CONV_HALO, :] = buf_ref[0]

    @pl.when(j > 0)
    def _():
        xp_ref[0:CONV_HALO, :] = xp_ref[tc:tc + CONV_HALO, :]

    u = glu_ref[:, 0:D_B] * jax.nn.sigmoid(glu_ref[:, D_B:2 * D_B])
    xp_ref[CONV_HALO:CONV_HALO + tc, :] = u

    @pl.when(j == pl.num_programs(1) - 1)
    def _():
        nbuf_ref[0] = xp_ref[CONV_HALO + tc - hist:CONV_HALO + tc, :]

    for r0 in range(0, tc, CONV_ROWS):
        acc = jnp.broadcast_to(b_ref[...], (CONV_ROWS, D_B))
        for tap in range(CONV_WIDTH):
            start = CONV_HALO - hist + r0 + tap
            acc = acc + w_ref[tap:tap + 1, :] * xp_ref[start:start + CONV_ROWS, :]
        mu = jnp.mean(acc, axis=-1, keepdims=True)
        cen = acc - mu
        var = jnp.mean(cen * cen, axis=-1, keepdims=True)
        z = cen * lax.rsqrt(var + EPS) * lng_ref[...] + lnb_ref[...]
        y_ref[r0:r0 + CONV_ROWS, :] = (z * jax.nn.sigmoid(z)).astype(BF16)


def conv_group(proj, row0, bsz, t, tc, buf, w, b, ln_g, ln_b):
    nt = t // tc
    rb0 = row0 // tc
    const2 = lambda bb, j: (0, 0)
    y, nbuf = pl.pallas_call(
        _conv_kernel,
        grid=(bsz, nt),
        in_specs=[
            pl.BlockSpec((tc, 2 * D_B), lambda bb, j: (rb0 + bb * nt + j, CB_GLU)),
            pl.BlockSpec((1, CONV_WIDTH - 1, D_B), lambda bb, j: (bb, 0, 0)),
            pl.BlockSpec((CONV_WIDTH, D_B), const2),
            pl.BlockSpec((1, D_B), const2),
            pl.BlockSpec((1, D_B), const2),
            pl.BlockSpec((1, D_B), const2),
        ],
        out_specs=[
            pl.BlockSpec((tc, D_B), lambda bb, j: (bb * nt + j, 0)),
            pl.BlockSpec((1, CONV_WIDTH - 1, D_B), lambda bb, j: (bb, 0, 0)),
        ],
        out_shape=[
            jax.ShapeDtypeStruct((bsz * t, D_B), BF16),
            jax.ShapeDtypeStruct((bsz, CONV_WIDTH - 1, D_B), F32),
        ],
        scratch_shapes=[pltpu.VMEM((CONV_HALO + tc + CONV_HALO, D_B), F32)],
        compiler_params=_cparams(("parallel", "arbitrary")),
        name="conv_module",
    )(proj, buf, w, b.reshape(1, D_B), ln_g.reshape(1, D_B), ln_b.reshape(1, D_B))
    return y, nbuf


def _mla_prep_kernel(cq_ref, ckv_ref, krg_ref, cos_ref, sin_ref, qg_ref, kvg_ref, wuq_ref, wuk_ref,
                     qlat_ref, qrope_ref, ckv_out_ref, kr_out_ref, ckv_b_ref, kr_b_ref):
    half = C_ROPE // 2
    cq = _rms(cq_ref[...], qg_ref[...]).astype(BF16)
    qh = jnp.dot(cq, wuq_ref[...], preferred_element_type=F32)
    cos = cos_ref[...]
    sin = sin_ref[...]
    x1 = qh[:, D_C:D_C + C_HEADS * half]
    x2 = qh[:, D_C + C_HEADS * half:D_C + 2 * C_HEADS * half]
    r1 = (x1 * cos - x2 * sin) * Q_SCALE
    r2 = (x2 * cos + x1 * sin) * Q_SCALE
    for h in range(C_HEADS):
        qn = qh[:, h * C_NOPE:(h + 1) * C_NOPE].astype(BF16)
        ql = lax.dot_general(qn, wuk_ref[h], (((1,), (1,)), ((), ())), preferred_element_type=F32)
        qlat_ref[h] = (ql * Q_SCALE).astype(BF16)
        qrope_ref[h] = jnp.concatenate(
            [r1[:, h * half:(h + 1) * half], r2[:, h * half:(h + 1) * half]], axis=1).astype(BF16)
    ckv = _rms(ckv_ref[...], kvg_ref[...])
    ckv_out_ref[...] = ckv
    ckv_b_ref[...] = ckv.astype(BF16)
    k1 = krg_ref[:, 0:half]
    k2 = krg_ref[:, half:C_ROPE]
    c1 = cos[:, 0:half]
    s1 = sin[:, 0:half]
    kr = jnp.concatenate([k1 * c1 - k2 * s1, k2 * c1 + k1 * s1], axis=1)
    kr_out_ref[...] = kr
    kr_b_ref[...] = kr.astype(BF16)


def mla_prep(proj, cos_t, sin_t, q_norm_g, kv_norm_g, w_uq_r, w_uk_b, *, tq=256):
    n = proj.shape[0]
    const2 = lambda i: (0, 0)
    return pl.pallas_call(
        _mla_prep_kernel,
        grid=(n // tq,),
        in_specs=[
            pl.BlockSpec((tq, Q_RANK), lambda i: (i, CB_CQ)),
            pl.BlockSpec((tq, KV_RANK), lambda i: (i, CB_CKV)),
            pl.BlockSpec((tq, 128), lambda i: (i, CB_KRG)),
            pl.BlockSpec((tq, C_HEADS * C_ROPE // 2), lambda i: (i, 0)),
            pl.BlockSpec((tq, C_HEADS * C_ROPE // 2), lambda i: (i, 0)),
            pl.BlockSpec((1, Q_RANK), const2),
            pl.BlockSpec((1, KV_RANK), const2),
            pl.BlockSpec((Q_RANK, C_HEADS * (C_NOPE + C_ROPE)), const2),
            pl.BlockSpec((C_HEADS, KV_RANK, C_NOPE), lambda i: (0, 0, 0)),
        ],
        out_specs=[
            pl.BlockSpec((C_HEADS, tq, KV_RANK), lambda i: (0, i, 0)),
            pl.BlockSpec((C_HEADS, tq, C_ROPE), lambda i: (0, i, 0)),
            pl.BlockSpec((tq, KV_RANK), lambda i: (i, 0)),
            pl.BlockSpec((tq, C_ROPE), lambda i: (i, 0)),
            pl.BlockSpec((tq, KV_RANK), lambda i: (i, 0)),
            pl.BlockSpec((tq, C_ROPE), lambda i: (i, 0)),
        ],
        out_shape=[
            jax.ShapeDtypeStruct((C_HEADS, n, KV_RANK), BF16),
            jax.ShapeDtypeStruct((C_HEADS, n, C_ROPE), BF16),
            jax.ShapeDtypeStruct((n, KV_RANK), F32),
            jax.ShapeDtypeStruct((n, C_ROPE), F32),
            jax.ShapeDtypeStruct((n, KV_RANK), BF16),
            jax.ShapeDtypeStruct((n, C_ROPE), BF16),
        ],
        compiler_params=_cparams(("parallel",)),
        name="mla_prep",
    )(proj, proj, proj, cos_t, sin_t, q_norm_g.reshape(1, Q_RANK), kv_norm_g.reshape(1, KV_RANK), w_uq_r, w_uk_b)


ATT_SLAB = 256
LANES = 128
_NT = (((1,), (1,)), ((), ()))


def _attn_slab(q, qr, kcb, krb, bias, rows, m_ref, l_ref, acc_ref):
    s = (lax.dot_general(q, kcb, _NT, preferred_element_type=F32)
         + lax.dot_general(qr, krb, _NT, preferred_element_type=F32))
    if bias is not None:
        s = s + bias
    tiles = [s[:, t * LANES:(t + 1) * LANES] for t in range(s.shape[1] // LANES)]
    mx = functools.reduce(jnp.maximum, tiles)
    m_prev = m_ref[rows, :]
    m_new = jnp.maximum(m_prev, jnp.max(mx, axis=1, keepdims=True))
    alpha = jnp.exp2(m_prev - m_new)
    ps = [jnp.exp2(t - m_new) for t in tiles]
    l_ref[rows, :] = alpha * l_ref[rows, :] + functools.reduce(jnp.add, ps)
    p = jnp.concatenate(ps, axis=1).astype(BF16)
    pv = jnp.dot(p, kcb, preferred_element_type=F32)
    acc_ref[rows, :] = jnp.concatenate([alpha] * (KV_RANK // LANES), axis=1) * acc_ref[rows, :] + pv
    m_ref[rows, :] = m_new


def _attn_finish(acc_ref, l_ref, wuv_ref, y_ref, tq):
    for h in range(C_HEADS):
        rows = slice(h * tq, (h + 1) * tq)
        o = acc_ref[rows, :] / jnp.sum(l_ref[rows, :], axis=1, keepdims=True)
        y_ref[:, h * C_VDIM:(h + 1) * C_VDIM] = jnp.dot(
            o.astype(BF16), wuv_ref[h], preferred_element_type=F32).astype(BF16)


def _attn_init(m_ref, l_ref, acc_ref):
    m_ref[...] = jnp.full(m_ref.shape, NEG_INF, F32)
    l_ref[...] = jnp.zeros(l_ref.shape, F32)
    acc_ref[...] = jnp.zeros(acc_ref.shape, F32)


def _attn_prompt_kernel(ql_ref, qr_ref, k_ref, kr_ref, wuv_ref, y_ref, m_ref, l_ref, acc_ref, *, kb):
    tq = ql_ref.shape[1]
    i = pl.program_id(1)
    _attn_init(m_ref, l_ref, acc_ref)

    def block(start, bias):
        kcb = k_ref[pl.ds(start, kb), :]
        krb = kr_ref[pl.ds(start, kb), :]
        for h in range(C_HEADS):
            _attn_slab(ql_ref[h], qr_ref[h], kcb, krb, bias, slice(h * tq, (h + 1) * tq), m_ref, l_ref, acc_ref)

    diag = (i * tq) // kb

    def body(j, carry):
        block(pl.multiple_of(j * kb, kb), None)
        return carry

    lax.fori_loop(0, diag, body, 0)
    start = pl.multiple_of(diag * kb, kb)
    qchunk = (i * tq + lax.broadcasted_iota(jnp.int32, (tq, kb), 0)) // CHUNK
    kchunk = (start + lax.broadcasted_iota(jnp.int32, (tq, kb), 1)) // CHUNK
    block(start, jnp.where(kchunk <= qchunk, 0.0, NEG_INF))
    _attn_finish(acc_ref, l_ref, wuv_ref, y_ref, tq)


def attn_prompt(q_lat, q_rope, ckv_b, kr_b, w_uv_b, bsz, t, *, tq=ATT_SLAB, kb=512):
    nq = t // tq
    rows = C_HEADS * tq
    return pl.pallas_call(
        functools.partial(_attn_prompt_kernel, kb=kb),
        grid=(bsz, nq),
        in_specs=[
            pl.BlockSpec((C_HEADS, tq, KV_RANK), lambda b, i: (0, b * nq + i, 0)),
            pl.BlockSpec((C_HEADS, tq, C_ROPE), lambda b, i: (0, b * nq + i, 0)),
            pl.BlockSpec((t, KV_RANK), lambda b, i: (b, 0)),
            pl.BlockSpec((t, C_ROPE), lambda b, i: (b, 0)),
            pl.BlockSpec((C_HEADS, KV_RANK, C_VDIM), lambda b, i: (0, 0, 0)),
        ],
        out_specs=pl.BlockSpec((tq, D_C), lambda b, i: (b * nq + i, 0)),
        out_shape=jax.ShapeDtypeStruct((bsz * t, D_C), BF16),
        scratch_shapes=[pltpu.VMEM((rows, LANES), F32), pltpu.VMEM((rows, LANES), F32),
                        pltpu.VMEM((rows, KV_RANK), F32)],
        compiler_params=_cparams(("parallel", "arbitrary")),
        name="attn_prompt",
    )(q_lat, q_rope, ckv_b, kr_b, w_uv_b)


def _attn_sample_kernel(ql_ref, qr_ref, k_ref, kr_ref, ck_ref, ckr_ref, wuv_ref, y_ref, m_ref, l_ref, acc_ref,
                        *, kb):
    tq = ql_ref.shape[1]
    rows_total = C_HEADS * tq
    hps = ATT_SLAB // tq
    _attn_init(m_ref, l_ref, acc_ref)
    past = ck_ref.shape[1]

    def block(kcb, krb, bias):
        for sidx in range(rows_total // ATT_SLAB):
            q = ql_ref[sidx * hps:(sidx + 1) * hps].reshape(ATT_SLAB, KV_RANK)
            qr = qr_ref[sidx * hps:(sidx + 1) * hps].reshape(ATT_SLAB, C_ROPE)
            _attn_slab(q, qr, kcb, krb, bias, slice(sidx * ATT_SLAB, (sidx + 1) * ATT_SLAB), m_ref, l_ref, acc_ref)

    def body(j, carry):
        start = pl.multiple_of(j * kb, kb)
        block(ck_ref[0, pl.ds(start, kb), :].astype(BF16), ckr_ref[0, pl.ds(start, kb), :].astype(BF16), None)
        return carry

    lax.fori_loop(0, past // kb, body, 0)
    pad = LANES - tq
    kcb = jnp.concatenate([k_ref[...], jnp.zeros((pad, KV_RANK), BF16)], axis=0)
    krb = jnp.concatenate([kr_ref[...], jnp.zeros((pad, C_ROPE), BF16)], axis=0)
    bias = jnp.where(lax.broadcasted_iota(jnp.int32, (1, LANES), 1) < tq, 0.0, NEG_INF)
    block(kcb, krb, bias)
    _attn_finish(acc_ref, l_ref, wuv_ref, y_ref, tq)


def attn_sample(q_lat, q_rope, ckv_b, kr_b, cache_lat, cache_rope, layer, w_uv_b, row0, bsz, t, *, kb=1024):
    rb0 = row0 // t
    past = cache_lat.shape[2]
    rows = C_HEADS * t
    return pl.pallas_call(
        functools.partial(_attn_sample_kernel, kb=kb),
        grid=(bsz,),
        in_specs=[
            pl.BlockSpec((C_HEADS, t, KV_RANK), lambda b: (0, rb0 + b, 0)),
            pl.BlockSpec((C_HEADS, t, C_ROPE), lambda b: (0, rb0 + b, 0)),
            pl.BlockSpec((t, KV_RANK), lambda b: (rb0 + b, 0)),
            pl.BlockSpec((t, C_ROPE), lambda b: (rb0 + b, 0)),
            pl.BlockSpec((None, 1, past, KV_RANK), lambda b: (layer, b, 0, 0)),
            pl.BlockSpec((None, 1, past, C_ROPE), lambda b: (layer, b, 0, 0)),
            pl.BlockSpec((C_HEADS, KV_RANK, C_VDIM), lambda b: (0, 0, 0)),
        ],
        out_specs=pl.BlockSpec((t, D_C), lambda b: (b, 0)),
        out_shape=jax.ShapeDtypeStruct((bsz * t, D_C), BF16),
        scratch_shapes=[pltpu.VMEM((rows, LANES), F32), pltpu.VMEM((rows, LANES), F32),
                        pltpu.VMEM((rows, KV_RANK), F32)],
        compiler_params=_cparams(("parallel",)),
        name="attn_sample",
    )(q_lat, q_rope, ckv_b, kr_b, cache_lat, cache_rope, w_uv_b)


def _oproj_kernel(x_ref, ya_ref, yb_ref, yc_ref, wa_ref, wb_ref, wc_ref, o_ref):
    acc = x_ref[...]
    acc = acc + jnp.dot(ya_ref[...], wa_ref[...], preferred_element_type=F32)
    acc = acc + jnp.dot(yb_ref[...], wb_ref[...], preferred_element_type=F32)
    acc = acc + jnp.dot(yc_ref[...], wc_ref[...], preferred_element_type=F32)
    o_ref[...] = acc


def out_proj(x, ya, yb, yc, w_out_all, layer, *, tm=1024, tn=512):
    n, d = x.shape
    return pl.pallas_call(
        _oproj_kernel,
        grid=(n // tm, d // tn),
        in_specs=[
            pl.BlockSpec((tm, tn), lambda i, j: (i, j)),
            pl.BlockSpec((tm, D_A), lambda i, j: (i, 0)),
            pl.BlockSpec((tm, D_B), lambda i, j: (i, 0)),
            pl.BlockSpec((tm, D_C), lambda i, j: (i, 0)),
            pl.BlockSpec((None, D_A, tn), lambda i, j: (layer, 0, j)),
            pl.BlockSpec((None, D_B, tn), lambda i, j: (layer, 1, j)),
            pl.BlockSpec((None, D_C, tn), lambda i, j: (layer, 1, j)),
        ],
        out_specs=pl.BlockSpec((tm, tn), lambda i, j: (i, j)),
        out_shape=jax.ShapeDtypeStruct((n, d), F32),
        compiler_params=_cparams(("parallel", "arbitrary")),
        name="out_proj",
    )(x, ya, yb, yc, w_out_all, w_out_all, w_out_all)


STEP_SCALE = 2.0 ** 64


def _extract_max(cur):
    m = jnp.max(cur, axis=0, keepdims=True)
    return m, jnp.where(cur == m, NEG_INF, cur)


def _peer_select_kernel(q_ref, keys_ref, s2_ref, e2_ref, thr_ref, e1_ref):
    for h in range(PEER_HEADS):
        st = []
        top = []
        for z in range(2):
            qhz = q_ref[:, (2 * h + z) * N_KEYS:(2 * h + z + 1) * N_KEYS].astype(BF16)
            s = lax.dot_general(keys_ref[h, z], qhz, (((1,), (1,)), ((), ())),
                                preferred_element_type=F32)
            st.append(s)
            cur = s
            vals = []
            for _ in range(PEER_TOPK + 1):
                m, cur = _extract_max(cur)
                vals.append(m)
            top.append(vals)
        v1, v2 = top
        v2_16 = jnp.concatenate(v2[:PEER_TOPK], axis=0)
        v2_8 = v2_16[:8]
        rowid = lax.broadcasted_iota(jnp.int32, v2_8.shape, 0)
        pieces = [v1[0] + v2_16, v1[1] + v2_8]
        for a in range(2, 8):
            pieces.append(jnp.where(rowid < PEER_TOPK // (a + 1), v1[a] + v2_8, NEG_INF))
        pieces.append(jnp.concatenate(v1[8:PEER_TOPK], axis=0) + v2[0])
        cur = jnp.concatenate(pieces, axis=0)
        best, cur = _extract_max(cur)
        zsum = jnp.ones_like(best)
        m = best
        for _ in range(PEER_TOPK - 1):
            m, cur = _extract_max(cur)
            zsum = zsum + jnp.exp(m - best)
        nxt = jnp.maximum(jnp.max(cur, axis=0, keepdims=True),
                          jnp.maximum(v1[PEER_TOPK] + v2[0], v1[0] + v2[PEER_TOPK]))
        tau = 0.5 * (m + nxt)
        s2 = st[1] * STEP_SCALE
        e2 = jnp.exp(st[1] - top[1][0]) / zsum
        for t in range(s2.shape[1] // LANES):
            s2_ref[h, t] = s2[:, t * LANES:(t + 1) * LANES]
            e2_ref[h, t] = e2[:, t * LANES:(t + 1) * LANES]
        thr_ref[h] = (tau - st[0]) * STEP_SCALE
        e1_ref[h] = jnp.exp(st[0] - top[0][0])


def peer_select(q, keys_b, *, tp=256):
    n = q.shape[0]
    shp = jax.ShapeDtypeStruct((PEER_HEADS, N_KEYS, n), F32)
    spec = pl.BlockSpec((PEER_HEADS, N_KEYS, tp), lambda i: (0, 0, i))
    tshp = jax.ShapeDtypeStruct((PEER_HEADS, n // LANES, N_KEYS, LANES), F32)
    tspec = pl.BlockSpec((PEER_HEADS, tp // LANES, N_KEYS, LANES), lambda i: (0, i, 0, 0))
    return pl.pallas_call(
        _peer_select_kernel,
        grid=(n // tp,),
        in_specs=[
            pl.BlockSpec((tp, PEER_HEADS * D_KEY), lambda i: (i, 0)),
            pl.BlockSpec((PEER_HEADS, 2, N_KEYS, D_KEY // 2), lambda i: (0, 0, 0, 0)),
        ],
        out_specs=[tspec, tspec, spec, spec],
        out_shape=[tshp, tshp, shp, shp],
        compiler_params=_cparams(("parallel",)),
        name="peer_select",
    )(q, keys_b)


PEER_CHUNK = 256


def _peer_dense_kernel(x_ref, xnt_ref, s2_ref, e2_ref, thr_ref, e1_ref, u_ref, vt_ref, o_ref, wt_ref, acc_ref,
                       *, eb):
    j = pl.program_id(1)
    nchunk = eb // PEER_CHUNK
    per = PEER_CHUNK // N_KEYS

    @pl.when(j == 0)
    def _():
        acc_ref[...] = jnp.zeros(acc_ref.shape, F32)

    def scores(c):
        return jnp.dot(u_ref[c * PEER_CHUNK:(c + 1) * PEER_CHUNK, :], xnt_ref[...], preferred_element_type=F32)

    def gate_chunk(c, at):
        for ii in range(per):
            row = c * per + ii
            thr_rows = [thr_ref[h, row:row + 1, :] for h in range(PEER_HEADS)]
            e1_rows = [e1_ref[h, row:row + 1, :] for h in range(PEER_HEADS)]
            for t in range(xnt_ref.shape[1] // LANES):
                cols = slice(t * LANES, (t + 1) * LANES)
                g = None
                for h in range(PEER_HEADS):
                    term = (jnp.maximum(jnp.minimum(s2_ref[h, t] - thr_rows[h][:, cols], e2_ref[h, t]), 0.0)
                            * e1_rows[h][:, cols])
                    g = term if g is None else g + term
                a = at[ii * N_KEYS:(ii + 1) * N_KEYS, cols]
                act = 0.5 * a * (1.0 + lax.erf(a * (2.0 ** -0.5)))
                wt_ref[row * N_KEYS:(row + 1) * N_KEYS, cols] = (g * act).astype(BF16)

    at_next = scores(0)
    for c in range(nchunk):
        at = at_next
        if c + 1 < nchunk:
            at_next = scores(c + 1)
        gate_chunk(c, at)
        rows = slice(c * PEER_CHUNK, (c + 1) * PEER_CHUNK)
        acc_ref[...] += jnp.dot(vt_ref[:, rows], wt_ref[rows, :], preferred_element_type=F32)

    @pl.when(j == pl.num_programs(1) - 1)
    def _():
        o_ref[...] = x_ref[...] + acc_ref[...].T


def peer_dense(x, xnt, s2, e2, thr, e1, u_all, vt_all, layer, *, tp=512, eb=1024):
    n, d = x.shape
    nb = N_EXPERTS // eb
    once = pl.Buffered(1)
    gspec = pl.BlockSpec((PEER_HEADS, tp // LANES, N_KEYS, LANES), lambda i, j: (0, i, 0, 0), pipeline_mode=once)
    rspec = pl.BlockSpec((PEER_HEADS, eb // N_KEYS, tp), lambda i, j: (0, j, i))
    return pl.pallas_call(
        functools.partial(_peer_dense_kernel, eb=eb),
        grid=(n // tp, nb),
        in_specs=[
            pl.BlockSpec((tp, d), lambda i, j: (i, 0), pipeline_mode=once),
            pl.BlockSpec((d, tp), lambda i, j: (0, i), pipeline_mode=once),
            gspec, gspec, rspec, rspec,
            pl.BlockSpec((None, eb, d), lambda i, j: (layer, j, 0)),
            pl.BlockSpec((None, d, eb), lambda i, j: (layer, 0, j)),
        ],
        out_specs=pl.BlockSpec((tp, d), lambda i, j: (i, 0)),
        out_shape=jax.ShapeDtypeStruct((n, d), F32),
        scratch_shapes=[pltpu.VMEM((eb, tp), BF16), pltpu.VMEM((d, tp), F32)],
        compiler_params=_cparams(("parallel", "arbitrary")),
        name="peer_dense",
    )(x, xnt, s2, e2, thr, e1, u_all, vt_all)


def _final_norm_kernel(x_ref, g_ref, o_ref):
    o_ref[...] = _rms(x_ref[...], g_ref[...])


def final_norm(x, g, *, tm=512):
    n, d = x.shape
    spec = pl.BlockSpec((tm, d), lambda i: (i, 0))
    return pl.pallas_call(
        _final_norm_kernel,
        grid=(n // tm,),
        in_specs=[spec, pl.BlockSpec((1, d), lambda i: (0, 0))],
        out_specs=spec,
        out_shape=jax.ShapeDtypeStruct((n, d), F32),
        compiler_params=_cparams(("parallel",)),
        name="final_norm",
    )(x, g.reshape(1, d))


def _prep_w_in(w_in):
    qa, ka, va, oa, ig, fg, glu, cq, ckv, kr = jnp.split(w_in, SPLIT_IDX, axis=-1)
    pad = jnp.zeros(w_in.shape[:-1] + (PROJ_COLS - IN_COLS,), w_in.dtype)
    return jnp.concatenate([qa, ka, va, oa, glu, cq, ckv, kr, ig, fg, pad], axis=-1).astype(BF16)


def _prep_w_uq(w_uq):
    w = w_uq.reshape(w_uq.shape[:-1] + (C_HEADS, C_NOPE + C_ROPE))
    half = C_ROPE // 2
    flat = lambda a: a.reshape(a.shape[:-2] + (-1,))
    return jnp.concatenate(
        [flat(w[..., :C_NOPE]), flat(w[..., C_NOPE:C_NOPE + half]), flat(w[..., C_NOPE + half:])],
        axis=-1).astype(BF16)


def _rope_tables(pos):
    half = C_ROPE // 2
    inv = ROPE_BASE ** (-jnp.arange(half, dtype=jnp.float32) / half)
    ang = pos.astype(jnp.float32)[:, None] * inv[None, :]
    return jnp.tile(jnp.cos(ang), (1, C_HEADS)), jnp.tile(jnp.sin(ang), (1, C_HEADS))


def kernel(x_prompt, x_sample, state_mlstm_C, state_mlstm_n, state_mlstm_m, state_conv, cache_kv_latent, cache_k_rope, norm1_g, w_in, b_gate, mlstm_norm_g, conv_w, conv_b, conv_ln_g, conv_ln_b, q_norm_g, kv_norm_g, w_uq, w_uk, w_uv, w_out, norm2_g, peer_wq, peer_keys, peer_u, peer_v, final_g):
    bp, sp, _ = x_prompt.shape
    bs, ts, _ = x_sample.shape
    past = cache_kv_latent.shape[2]
    n_p = bp * sp
    n_s = bs * ts

    x = jnp.concatenate([x_prompt.reshape(n_p, D_MODEL), x_sample.reshape(n_s, D_MODEL)], axis=0)
    pos = jnp.concatenate([jnp.tile(jnp.arange(sp), bp), jnp.tile(past + jnp.arange(ts), bs)])
    cos_t, sin_t = _rope_tables(pos)

    w_in_b = _prep_w_in(w_in)
    w_uq_b = _prep_w_uq(w_uq)
    w_uk_b = w_uk.astype(BF16)
    w_uv_b = w_uv.astype(BF16)
    w_out_b = w_out.astype(BF16)
    peer_wq_b = peer_wq.astype(BF16)
    peer_keys_b = peer_keys.astype(BF16)
    peer_u_b = peer_u.astype(BF16)
    peer_vt_b = jnp.swapaxes(peer_v, 1, 2).astype(BF16)
    gate_bias = jnp.pad(b_gate, ((0, 0), (GATE_LANE, 128 - GATE_LANE - 2 * A_HEADS)))[:, None, :]

    zc = jnp.zeros((bp, A_HEADS, A_HEAD_DIM, A_HEAD_DIM), F32)
    zn = jnp.zeros((bp, A_HEADS, A_HEAD_DIM), F32)
    zm = jnp.zeros((bp, A_HEADS), F32)
    zbuf = jnp.zeros((bp, CONV_WIDTH - 1, D_B), F32)

    p_st = [[] for _ in range(6)]
    s_st = [[] for _ in range(6)]
    for l in range(DEPTH):
        proj, _ = norm_matmul(x, norm1_g[l], w_in_b, l)
        ya_p, c_p, nn_p, m_p = mlstm_group(proj, 0, bp, sp, CHUNK, zc, zn, zm, gate_bias[l], mlstm_norm_g[l])
        ya_s, c_s, nn_s, m_s = mlstm_group(proj, n_p, bs, ts, ts, state_mlstm_C[l], state_mlstm_n[l],
                                           state_mlstm_m[l], gate_bias[l], mlstm_norm_g[l])
        yb_p, buf_p = conv_group(proj, 0, bp, sp, 256, zbuf, conv_w[l], conv_b[l], conv_ln_g[l], conv_ln_b[l])
        yb_s, buf_s = conv_group(proj, n_p, bs, ts, ts, state_conv[l], conv_w[l], conv_b[l], conv_ln_g[l],
                                 conv_ln_b[l])
        q_lat, q_rope, ckv, kr, ckv_b, kr_b = mla_prep(proj, cos_t, sin_t, q_norm_g[l], kv_norm_g[l], w_uq_b[l],
                                                       w_uk_b[l])
        yc_p = attn_prompt(q_lat, q_rope, ckv_b, kr_b, w_uv_b[l], bp, sp)
        yc_s = attn_sample(q_lat, q_rope, ckv_b, kr_b, cache_kv_latent, cache_k_rope, l, w_uv_b[l], n_p, bs, ts)
        ya = jnp.concatenate([ya_p, ya_s], axis=0)
        yb = jnp.concatenate([yb_p, yb_s], axis=0)
        yc = jnp.concatenate([yc_p, yc_s], axis=0)
        x = out_proj(x, ya, yb, yc, w_out_b, l)
        q, xnt = norm_matmul(x, norm2_g[l], peer_wq_b, l, emit_xn=True)
        s2, e2, thr, e1 = peer_select(q, peer_keys_b[l])
        x = peer_dense(x, xnt, s2, e2, thr, e1, peer_u_b, peer_vt_b, l)

        for lst, vals in ((p_st, (c_p, nn_p, m_p, buf_p, ckv[:n_p].reshape(bp, sp, KV_RANK),
                                  kr[:n_p].reshape(bp, sp, C_ROPE))),
                          (s_st, (c_s, nn_s, m_s, buf_s, ckv[n_p:].reshape(bs, ts, KV_RANK),
                                  kr[n_p:].reshape(bs, ts, C_ROPE)))):
            for i in range(6):
                lst[i].append(vals[i])

    y = final_norm(x, final_g)
    y_prompt = y[:n_p].reshape(bp, sp, D_MODEL)
    y_sample = y[n_p:].reshape(bs, ts, D_MODEL)
    return (y_prompt, y_sample, *[jnp.stack(a) for a in p_st], *[jnp.stack(a) for a in s_st])
```

```python
import functools

import numpy as np
import jax
import jax.numpy as jnp
from jax import lax
from jax.experimental import pallas as pl
from jax.experimental.pallas import tpu as pltpu

F32 = jnp.float32
BF16 = jnp.bfloat16
HIGHEST = lax.Precision.HIGHEST

D_MODEL = 2048
DEPTH = 4
CHUNK = 64
EPS = 1e-6
A_HEADS = 4
A_HEAD_DIM = 128
D_A = A_HEADS * A_HEAD_DIM
D_B = 512
CONV_WIDTH = 31
C_HEADS = 8
C_NOPE = 128
C_ROPE = 64
C_VDIM = 128
Q_RANK = 512
KV_RANK = 256
D_C = C_HEADS * C_VDIM
ROPE_BASE = 10000.0
ATTN_SCALE = (C_NOPE + C_ROPE) ** -0.5
Q_SCALE = ATTN_SCALE * float(np.log2(np.e))
SPLIT_SIZES = (D_A, D_A, D_A, D_A, A_HEADS, A_HEADS, 2 * D_B, Q_RANK, KV_RANK, C_ROPE)
SPLIT_IDX = tuple(int(i) for i in np.cumsum(SPLIT_SIZES)[:-1])
IN_COLS = int(sum(SPLIT_SIZES))
PEER_HEADS = 8
N_KEYS = 128
N_EXPERTS = N_KEYS * N_KEYS
D_KEY = 256
PEER_TOPK = 16

PROJ_COLS = 4096
GATE_LANE = C_ROPE
CB_Q, CB_K, CB_V, CB_O = 0, 1, 2, 3
CB_GLU = 2
CB_CQ = 6
CB_CKV = 14
CB_KRG = 30

V7X_VMEM_LIMIT = 60 * 1024 * 1024
NEG_INF = float("-inf")


def _cparams(sem):
    return pltpu.CompilerParams(dimension_semantics=sem, vmem_limit_bytes=V7X_VMEM_LIMIT)


def _rms(x, g):
    return x * lax.rsqrt(jnp.mean(x * x, axis=-1, keepdims=True) + EPS) * g


def _log_sigmoid(x):
    return jnp.minimum(x, 0.0) - jnp.log1p(jnp.exp(-jnp.abs(x)))


def _norm_matmul_kernel(x_ref, g_ref, w_ref, o_ref, *rest, emit_xn):
    xn_out_ref = rest[0] if emit_xn else None
    xn_ref = rest[-1]

    @pl.when(pl.program_id(1) == 0)
    def _():
        yb = _rms(x_ref[...], g_ref[...]).astype(BF16)
        xn_ref[...] = yb
        if emit_xn:
            xn_out_ref[...] = yb.T

    o_ref[...] = jnp.dot(xn_ref[...], w_ref[...], preferred_element_type=F32)


def norm_matmul(x, g, w_all, layer, *, emit_xn=False, tm=1024, tn=1024):
    n, d = x.shape
    m = w_all.shape[2]
    out_shape = [jax.ShapeDtypeStruct((n, m), F32)]
    out_specs = [pl.BlockSpec((tm, tn), lambda i, j: (i, j))]
    if emit_xn:
        out_shape.append(jax.ShapeDtypeStruct((d, n), BF16))
        out_specs.append(pl.BlockSpec((d, tm), lambda i, j: (0, i)))
    outs = pl.pallas_call(
        functools.partial(_norm_matmul_kernel, emit_xn=emit_xn),
        grid=(n // tm, m // tn),
        in_specs=[
            pl.BlockSpec((tm, d), lambda i, j: (i, 0)),
            pl.BlockSpec((1, d), lambda i, j: (0, 0)),
            pl.BlockSpec((None, d, tn), lambda i, j: (layer, 0, j)),
        ],
        out_specs=out_specs,
        out_shape=out_shape,
        scratch_shapes=[pltpu.VMEM((tm, d), BF16)],
        compiler_params=_cparams(("parallel", "arbitrary")),
        name="norm_matmul",
    )(x, g.reshape(1, d), w_all)
    return (outs[0], outs[1]) if emit_xn else (outs[0], None)


def _mlstm_kernel(q_ref, k_ref, v_ref, o_ref, g_ref, c0_ref, n0_ref, m0_ref, bias_ref, ng_ref,
                  y_ref, c_ref, n_ref, m_ref):
    L = q_ref.shape[0]

    @pl.when(pl.program_id(1) == 0)
    def _():
        c_ref[...] = c0_ref[...]
        n_ref[...] = n0_ref[...]
        m_ref[...] = m0_ref[...]

    pre = g_ref[...] + bias_ref[...]
    lf_all = _log_sigmoid(pre)
    row = lax.broadcasted_iota(jnp.int32, (L, L), 0)
    col = lax.broadcasted_iota(jnp.int32, (L, L), 1)
    causal = col <= row
    tril = causal.astype(F32)
    triu = (row <= col).astype(F32)
    b_all = jnp.dot(tril, lf_all, precision=HIGHEST, preferred_element_type=F32)
    sel = (lax.broadcasted_iota(jnp.int32, (8, 128), 1)
           == lax.broadcasted_iota(jnp.int32, (8, 128), 0) + GATE_LANE).astype(F32)
    rows = lax.dot_general(sel, pre, (((1,), (1,)), ((), ())), precision=HIGHEST,
                           preferred_element_type=F32)
    ig_rows = rows[0:A_HEADS]
    lf_rows = _log_sigmoid(rows[A_HEADS:2 * A_HEADS])
    b_rows = jnp.dot(lf_rows, triu, precision=HIGHEST, preferred_element_type=F32)

    for h in range(A_HEADS):
        sl = slice(h * A_HEAD_DIM, (h + 1) * A_HEAD_DIM)
        q = q_ref[:, sl]
        k = k_ref[:, sl] * (A_HEAD_DIM ** -0.5)
        v = v_ref[:, sl]
        qb, kb, vb = q.astype(BF16), k.astype(BF16), v.astype(BF16)
        b_col = b_all[:, GATE_LANE + A_HEADS + h:GATE_LANE + A_HEADS + h + 1]
        ig_col = pre[:, GATE_LANE + h:GATE_LANE + h + 1]
        b_row = b_rows[h:h + 1]
        ig_row = ig_rows[h:h + 1]
        m_prev = m_ref[0, h:h + 1, 0:1]
        c_prev = c_ref[0, h]
        n_prev = n_ref[0, h:h + 1, :]

        dm = jnp.where(causal, b_col - b_row + ig_row, NEG_INF)
        inter = b_col + m_prev
        m_t = jnp.maximum(inter, jnp.max(dm, axis=1, keepdims=True))
        w_intra = jnp.exp(dm - m_t)
        w_inter = jnp.exp(inter - m_t)
        s = lax.dot_general(qb, kb, (((1,), (1,)), ((), ())), preferred_element_type=F32) * w_intra
        num = (jnp.dot(s.astype(BF16), vb, preferred_element_type=F32)
               + w_inter * jnp.dot(qb, c_prev.astype(BF16), preferred_element_type=F32))
        den = jnp.sum(s, axis=1, keepdims=True) + w_inter * jnp.sum(q * n_prev, axis=1, keepdims=True)
        hh = num / jnp.maximum(jnp.abs(den), jnp.exp(-m_t))

        m_new = m_t[L - 1:L, :]
        b_last = b_col[L - 1:L, :]
        decay = jnp.exp(b_last + m_prev - m_new)
        wk = jnp.exp(b_last - b_col + ig_col - m_new)
        kw = k * wk
        c_ref[0, h] = decay * c_prev + lax.dot_general(
            kw.astype(BF16), vb, (((0,), (0,)), ((), ())), preferred_element_type=F32)
        n_ref[0, h:h + 1, :] = decay * n_prev + jnp.sum(kw, axis=0, keepdims=True)
        m_ref[0, h:h + 1, :] = jnp.broadcast_to(m_new, (1, 128))

        hn = _rms(hh, ng_ref[:, sl])
        y_ref[:, sl] = (jax.nn.sigmoid(o_ref[:, sl]) * hn).astype(BF16)


def mlstm_group(proj, row0, bsz, t, chunk_len, c0, n0, m0, gate_bias, norm_g):
    nc = t // chunk_len
    rb0 = row0 // chunk_len

    def col_spec(width, cb):
        return pl.BlockSpec((chunk_len, width), lambda b, c: (rb0 + b * nc + c, cb))

    st4 = lambda b, c: (b, 0, 0, 0)
    st3 = lambda b, c: (b, 0, 0)
    m0p = jnp.pad(jnp.broadcast_to(m0[:, :, None], (bsz, A_HEADS, 128)), ((0, 0), (0, 8 - A_HEADS), (0, 0)))
    y, c_f, n_f, m_f = pl.pallas_call(
        _mlstm_kernel,
        grid=(bsz, nc),
        in_specs=[
            col_spec(D_A, CB_Q), col_spec(D_A, CB_K), col_spec(D_A, CB_V), col_spec(D_A, CB_O),
            col_spec(128, CB_KRG),
            pl.BlockSpec((1, A_HEADS, A_HEAD_DIM, A_HEAD_DIM), st4),
            pl.BlockSpec((1, A_HEADS, A_HEAD_DIM), st3),
            pl.BlockSpec((1, 8, 128), st3),
            pl.BlockSpec((1, 128), lambda b, c: (0, 0)),
            pl.BlockSpec((1, D_A), lambda b, c: (0, 0)),
        ],
        out_specs=[
            pl.BlockSpec((chunk_len, D_A), lambda b, c: (b * nc + c, 0)),
            pl.BlockSpec((1, A_HEADS, A_HEAD_DIM, A_HEAD_DIM), st4),
            pl.BlockSpec((1, A_HEADS, A_HEAD_DIM), st3),
            pl.BlockSpec((1, 8, 128), st3),
        ],
        out_shape=[
            jax.ShapeDtypeStruct((bsz * t, D_A), BF16),
            jax.ShapeDtypeStruct((bsz, A_HEADS, A_HEAD_DIM, A_HEAD_DIM), F32),
            jax.ShapeDtypeStruct((bsz, A_HEADS, A_HEAD_DIM), F32),
            jax.ShapeDtypeStruct((bsz, 8, 128), F32),
        ],
        compiler_params=_cparams(("parallel", "arbitrary")),
        name="mlstm",
    )(proj, proj, proj, proj, proj, c0, n0, m0p, gate_bias, norm_g.reshape(1, D_A))
    return y, c_f, n_f, m_f[:, :A_HEADS, 0]


CONV_HALO = 32
CONV_ROWS = 32


def _conv_kernel(glu_ref, buf_ref, w_ref, b_ref, lng_ref, lnb_ref, y_ref, nbuf_ref, xp_ref):
    tc = glu_ref.shape[0]
    j = pl.program_id(1)
    hist = CONV_WIDTH - 1

    @pl.when(j == 0)
    def _():
        xp_ref[0:CONV_HALO - hist, :] = jnp.zeros((CONV_HALO - hist, D_B), F32)
        xp_ref[CONV_HALO - hist:CONV_HALO, :] = buf_ref[0]

    @pl.when(j > 0)
    def _():
        xp_ref[0:CONV_HALO, :] = xp_ref[tc:tc + CONV_HALO, :]

    u = glu_ref[:, 0:D_B] * jax.nn.sigmoid(glu_ref[:, D_B:2 * D_B])
    xp_ref[CONV_HALO:CONV_HALO + tc, :] = u

    @pl.when(j == pl.num_programs(1) - 1)
    def _():
        nbuf_ref[0] = xp_ref[CONV_HALO + tc - hist:CONV_HALO + tc, :]

    for r0 in range(0, tc, CONV_ROWS):
        acc = jnp.broadcast_to(b_ref[...], (CONV_ROWS, D_B))
        for tap in range(CONV_WIDTH):
            start = CONV_HALO - hist + r0 + tap
            acc = acc + w_ref[tap:tap + 1, :] * xp_ref[start:start + CONV_ROWS, :]
        mu = jnp.mean(acc, axis=-1, keepdims=True)
        cen = acc - mu
        var = jnp.mean(cen * cen, axis=-1, keepdims=True)
        z = cen * lax.rsqrt(var + EPS) * lng_ref[...] + lnb_ref[...]
        y_ref[r0:r0 + CONV_ROWS, :] = (z * jax.nn.sigmoid(z)).astype(BF16)


def conv_group(proj, row0, bsz, t, tc, buf, w, b, ln_g, ln_b):
    nt = t // tc
    rb0 = row0 // tc
    const2 = lambda bb, j: (0, 0)
    y, nbuf = pl.pallas_call(
        _conv_kernel,
        grid=(bsz, nt),
        in_specs=[
            pl.BlockSpec((tc, 2 * D_B), lambda bb, j: (rb0 + bb * nt + j, CB_GLU)),
            pl.BlockSpec((1, CONV_WIDTH - 1, D_B), lambda bb, j: (bb, 0, 0)),
            pl.BlockSpec((CONV_WIDTH, D_B), const2),
            pl.BlockSpec((1, D_B), const2),
            pl.BlockSpec((1, D_B), const2),
            pl.BlockSpec((1, D_B), const2),
        ],
        out_specs=[
            pl.BlockSpec((tc, D_B), lambda bb, j: (bb * nt + j, 0)),
            pl.BlockSpec((1, CONV_WIDTH - 1, D_B), lambda bb, j: (bb, 0, 0)),
        ],
        out_shape=[
            jax.ShapeDtypeStruct((bsz * t, D_B), BF16),
            jax.ShapeDtypeStruct((bsz, CONV_WIDTH - 1, D_B), F32),
        ],
        scratch_shapes=[pltpu.VMEM((CONV_HALO + tc + CONV_HALO, D_B), F32)],
        compiler_params=_cparams(("parallel", "arbitrary")),
        name="conv_module",
    )(proj, buf, w, b.reshape(1, D_B), ln_g.reshape(1, D_B), ln_b.reshape(1, D_B))
    return y, nbuf


def _mla_prep_kernel(cq_ref, ckv_ref, krg_ref, cos_ref, sin_ref, qg_ref, kvg_ref, wuq_ref, wuk_ref,
                     qlat_ref, qrope_ref, ckv_out_ref, kr_out_ref, ckv_b_ref, kr_b_ref):
    half = C_ROPE // 2
    cq = _rms(cq_ref[...], qg_ref[...]).astype(BF16)
    qh = jnp.dot(cq, wuq_ref[...], preferred_element_type=F32)
    cos = cos_ref[...]
    sin = sin_ref[...]
    x1 = qh[:, D_C:D_C + C_HEADS * half]
    x2 = qh[:, D_C + C_HEADS * half:D_C + 2 * C_HEADS * half]
    r1 = (x1 * cos - x2 * sin) * Q_SCALE
    r2 = (x2 * cos + x1 * sin) * Q_SCALE
    for h in range(C_HEADS):
        qn = qh[:, h * C_NOPE:(h + 1) * C_NOPE].astype(BF16)
        ql = lax.dot_general(qn, wuk_ref[h], (((1,), (1,)), ((), ())), preferred_element_type=F32)
        qlat_ref[h] = (ql * Q_SCALE).astype(BF16)
        qrope_ref[h] = jnp.concatenate(
            [r1[:, h * half:(h + 1) * half], r2[:, h * half:(h + 1) * half]], axis=1).astype(BF16)
    ckv = _rms(ckv_ref[...], kvg_ref[...])
    ckv_out_ref[...] = ckv
    ckv_b_ref[...] = ckv.astype(BF16)
    k1 = krg_ref[:, 0:half]
    k2 = krg_ref[:, half:C_ROPE]
    c1 = cos[:, 0:half]
    s1 = sin[:, 0:half]
    kr = jnp.concatenate([k1 * c1 - k2 * s1, k2 * c1 + k1 * s1], axis=1)
    kr_out_ref[...] = kr
    kr_b_ref[...] = kr.astype(BF16)


def mla_prep(proj, cos_t, sin_t, q_norm_g, kv_norm_g, w_uq_r, w_uk_b, *, tq=256):
    n = proj.shape[0]
    const2 = lambda i: (0, 0)
    return pl.pallas_call(
        _mla_prep_kernel,
        grid=(n // tq,),
        in_specs=[
            pl.BlockSpec((tq, Q_RANK), lambda i: (i, CB_CQ)),
            pl.BlockSpec((tq, KV_RANK), lambda i: (i, CB_CKV)),
            pl.BlockSpec((tq, 128), lambda i: (i, CB_KRG)),
            pl.BlockSpec((tq, C_HEADS * C_ROPE // 2), lambda i: (i, 0)),
            pl.BlockSpec((tq, C_HEADS * C_ROPE // 2), lambda i: (i, 0)),
            pl.BlockSpec((1, Q_RANK), const2),
            pl.BlockSpec((1, KV_RANK), const2),
            pl.BlockSpec((Q_RANK, C_HEADS * (C_NOPE + C_ROPE)), const2),
            pl.BlockSpec((C_HEADS, KV_RANK, C_NOPE), lambda i: (0, 0, 0)),
        ],
        out_specs=[
            pl.BlockSpec((C_HEADS, tq, KV_RANK), lambda i: (0, i, 0)),
            pl.BlockSpec((C_HEADS, tq, C_ROPE), lambda i: (0, i, 0)),
            pl.BlockSpec((tq, KV_RANK), lambda i: (i, 0)),
            pl.BlockSpec((tq, C_ROPE), lambda i: (i, 0)),
            pl.BlockSpec((tq, KV_RANK), lambda i: (i, 0)),
            pl.BlockSpec((tq, C_ROPE), lambda i: (i, 0)),
        ],
        out_shape=[
            jax.ShapeDtypeStruct((C_HEADS, n, KV_RANK), BF16),
            jax.ShapeDtypeStruct((C_HEADS, n, C_ROPE), BF16),
            jax.ShapeDtypeStruct((n, KV_RANK), F32),
            jax.ShapeDtypeStruct((n, C_ROPE), F32),
            jax.ShapeDtypeStruct((n, KV_RANK), BF16),
            jax.ShapeDtypeStruct((n, C_ROPE), BF16),
        ],
        compiler_params=_cparams(("parallel",)),
        name="mla_prep",
    )(proj, proj, proj, cos_t, sin_t, q_norm_g.reshape(1, Q_RANK), kv_norm_g.reshape(1, KV_RANK), w_uq_r, w_uk_b)


ATT_SLAB = 256
LANES = 128
_NT = (((1,), (1,)), ((), ()))


def _attn_slab(q, qr, kcb, krb, bias, rows, m_ref, l_ref, acc_ref):
    s = (lax.dot_general(q, kcb, _NT, preferred_element_type=F32)
         + lax.dot_general(qr, krb, _NT, preferred_element_type=F32))
    if bias is not None:
        s = s + bias
    tiles = [s[:, t * LANES:(t + 1) * LANES] for t in range(s.shape[1] // LANES)]
    mx = functools.reduce(jnp.maximum, tiles)
    m_prev = m_ref[rows, :]
    m_new = jnp.maximum(m_prev, jnp.max(mx, axis=1, keepdims=True))
    alpha = jnp.exp2(m_prev - m_new)
    ps = [jnp.exp2(t - m_new) for t in tiles]
    l_ref[rows, :] = alpha * l_ref[rows, :] + functools.reduce(jnp.add, ps)
    p = jnp.concatenate(ps, axis=1).astype(BF16)
    pv = jnp.dot(p, kcb, preferred_element_type=F32)
    acc_ref[rows, :] = jnp.concatenate([alpha] * (KV_RANK // LANES), axis=1) * acc_ref[rows, :] + pv
    m_ref[rows, :] = m_new


def _attn_finish(acc_ref, l_ref, wuv_ref, y_ref, tq):
    for h in range(C_HEADS):
        rows = slice(h * tq, (h + 1) * tq)
        o = acc_ref[rows, :] / jnp.sum(l_ref[rows, :], axis=1, keepdims=True)
        y_ref[:, h * C_VDIM:(h + 1) * C_VDIM] = jnp.dot(
            o.astype(BF16), wuv_ref[h], preferred_element_type=F32).astype(BF16)


def _attn_init(m_ref, l_ref, acc_ref):
    m_ref[...] = jnp.full(m_ref.shape, NEG_INF, F32)
    l_ref[...] = jnp.zeros(l_ref.shape, F32)
    acc_ref[...] = jnp.zeros(acc_ref.shape, F32)


def _attn_prompt_kernel(ql_ref, qr_ref, k_ref, kr_ref, wuv_ref, y_ref, m_ref, l_ref, acc_ref, *, kb):
    tq = ql_ref.shape[1]
    i = pl.program_id(1)
    _attn_init(m_ref, l_ref, acc_ref)

    def block(start, bias):
        kcb = k_ref[pl.ds(start, kb), :]
        krb = kr_ref[pl.ds(start, kb), :]
        for h in range(C_HEADS):
            _attn_slab(ql_ref[h], qr_ref[h], kcb, krb, bias, slice(h * tq, (h + 1) * tq), m_ref, l_ref, acc_ref)

    diag = (i * tq) // kb

    def body(j, carry):
        block(pl.multiple_of(j * kb, kb), None)
        return carry

    lax.fori_loop(0, diag, body, 0)
    start = pl.multiple_of(diag * kb, kb)
    qchunk = (i * tq + lax.broadcasted_iota(jnp.int32, (tq, kb), 0)) // CHUNK
    kchunk = (start + lax.broadcasted_iota(jnp.int32, (tq, kb), 1)) // CHUNK
    block(start, jnp.where(kchunk <= qchunk, 0.0, NEG_INF))
    _attn_finish(acc_ref, l_ref, wuv_ref, y_ref, tq)


def attn_prompt(q_lat, q_rope, ckv_b, kr_b, w_uv_b, bsz, t, *, tq=ATT_SLAB, kb=512):
    nq = t // tq
    rows = C_HEADS * tq
    return pl.pallas_call(
        functools.partial(_attn_prompt_kernel, kb=kb),
        grid=(bsz, nq),
        in_specs=[
            pl.BlockSpec((C_HEADS, tq, KV_RANK), lambda b, i: (0, b * nq + i, 0)),
            pl.BlockSpec((C_HEADS, tq, C_ROPE), lambda b, i: (0, b * nq + i, 0)),
            pl.BlockSpec((t, KV_RANK), lambda b, i: (b, 0)),
            pl.BlockSpec((t, C_ROPE), lambda b, i: (b, 0)),
            pl.BlockSpec((C_HEADS, KV_RANK, C_VDIM), lambda b, i: (0, 0, 0)),
        ],
        out_specs=pl.BlockSpec((tq, D_C), lambda b, i: (b * nq + i, 0)),
        out_shape=jax.ShapeDtypeStruct((bsz * t, D_C), BF16),
        scratch_shapes=[pltpu.VMEM((rows, LANES), F32), pltpu.VMEM((rows, LANES), F32),
                        pltpu.VMEM((rows, KV_RANK), F32)],
        compiler_params=_cparams(("parallel", "arbitrary")),
        name="attn_prompt",
    )(q_lat, q_rope, ckv_b, kr_b, w_uv_b)


def _attn_sample_kernel(ql_ref, qr_ref, k_ref, kr_ref, ck_ref, ckr_ref, wuv_ref, y_ref, m_ref, l_ref, acc_ref,
                        *, kb):
    tq = ql_ref.shape[1]
    rows_total = C_HEADS * tq
    hps = ATT_SLAB // tq
    _attn_init(m_ref, l_ref, acc_ref)
    past = ck_ref.shape[1]

    def block(kcb, krb, bias):
        for sidx in range(rows_total // ATT_SLAB):
            q = ql_ref[sidx * hps:(sidx + 1) * hps].reshape(ATT_SLAB, KV_RANK)
            qr = qr_ref[sidx * hps:(sidx + 1) * hps].reshape(ATT_SLAB, C_ROPE)
            _attn_slab(q, qr, kcb, krb, bias, slice(sidx * ATT_SLAB, (sidx + 1) * ATT_SLAB), m_ref, l_ref, acc_ref)

    def body(j, carry):
        start = pl.multiple_of(j * kb, kb)
        block(ck_ref[0, pl.ds(start, kb), :].astype(BF16), ckr_ref[0, pl.ds(start, kb), :].astype(BF16), None)
        return carry

    lax.fori_loop(0, past // kb, body, 0)
    pad = LANES - tq
    kcb = jnp.concatenate([k_ref[...], jnp.zeros((pad, KV_RANK), BF16)], axis=0)
    krb = jnp.concatenate([kr_ref[...], jnp.zeros((pad, C_ROPE), BF16)], axis=0)
    bias = jnp.where(lax.broadcasted_iota(jnp.int32, (1, LANES), 1) < tq, 0.0, NEG_INF)
    block(kcb, krb, bias)
    _attn_finish(acc_ref, l_ref, wuv_ref, y_ref, tq)


def attn_sample(q_lat, q_rope, ckv_b, kr_b, cache_lat, cache_rope, layer, w_uv_b, row0, bsz, t, *, kb=1024):
    rb0 = row0 // t
    past = cache_lat.shape[2]
    rows = C_HEADS * t
    return pl.pallas_call(
        functools.partial(_attn_sample_kernel, kb=kb),
        grid=(bsz,),
        in_specs=[
            pl.BlockSpec((C_HEADS, t, KV_RANK), lambda b: (0, rb0 + b, 0)),
            pl.BlockSpec((C_HEADS, t, C_ROPE), lambda b: (0, rb0 + b, 0)),
            pl.BlockSpec((t, KV_RANK), lambda b: (rb0 + b, 0)),
            pl.BlockSpec((t, C_ROPE), lambda b: (rb0 + b, 0)),
            pl.BlockSpec((None, 1, past, KV_RANK), lambda b: (layer, b, 0, 0)),
            pl.BlockSpec((None, 1, past, C_ROPE), lambda b: (layer, b, 0, 0)),
            pl.BlockSpec((C_HEADS, KV_RANK, C_VDIM), lambda b: (0, 0, 0)),
        ],
        out_specs=pl.BlockSpec((t, D_C), lambda b: (b, 0)),
        out_shape=jax.ShapeDtypeStruct((bsz * t, D_C), BF16),
        scratch_shapes=[pltpu.VMEM((rows, LANES), F32), pltpu.VMEM((rows, LANES), F32),
                        pltpu.VMEM((rows, KV_RANK), F32)],
        compiler_params=_cparams(("parallel",)),
        name="attn_sample",
    )(q_lat, q_rope, ckv_b, kr_b, cache_lat, cache_rope, w_uv_b)


def _oproj_kernel(x_ref, ya_ref, yb_ref, yc_ref, wa_ref, wb_ref, wc_ref, o_ref):
    acc = x_ref[...]
    acc = acc + jnp.dot(ya_ref[...], wa_ref[...], preferred_element_type=F32)
    acc = acc + jnp.dot(yb_ref[...], wb_ref[...], preferred_element_type=F32)
    acc = acc + jnp.dot(yc_ref[...], wc_ref[...], preferred_element_type=F32)
    o_ref[...] = acc


def out_proj(x, ya, yb, yc, w_out_all, layer, *, tm=1024, tn=512):
    n, d = x.shape
    return pl.pallas_call(
        _oproj_kernel,
        grid=(n // tm, d // tn),
        in_specs=[
            pl.BlockSpec((tm, tn), lambda i, j: (i, j)),
            pl.BlockSpec((tm, D_A), lambda i, j: (i, 0)),
            pl.BlockSpec((tm, D_B), lambda i, j: (i, 0)),
            pl.BlockSpec((tm, D_C), lambda i, j: (i, 0)),
            pl.BlockSpec((None, D_A, tn), lambda i, j: (layer, 0, j)),
            pl.BlockSpec((None, D_B, tn), lambda i, j: (layer, 1, j)),
            pl.BlockSpec((None, D_C, tn), lambda i, j: (layer, 1, j)),
        ],
        out_specs=pl.BlockSpec((tm, tn), lambda i, j: (i, j)),
        out_shape=jax.ShapeDtypeStruct((n, d), F32),
        compiler_params=_cparams(("parallel", "arbitrary")),
        name="out_proj",
    )(x, ya, yb, yc, w_out_all, w_out_all, w_out_all)


STEP_SCALE = 2.0 ** 64


def _extract_max(cur):
    m = jnp.max(cur, axis=0, keepdims=True)
    return m, jnp.where(cur == m, NEG_INF, cur)


def _peer_select_kernel(q_ref, keys_ref, se_ref, thr_ref, e1_ref):
    for h in range(PEER_HEADS):
        st = []
        top = []
        for z in range(2):
            qhz = q_ref[:, (2 * h + z) * N_KEYS:(2 * h + z + 1) * N_KEYS].astype(BF16)
            s = lax.dot_general(keys_ref[h, z], qhz, (((1,), (1,)), ((), ())),
                                preferred_element_type=F32)
            st.append(s)
            cur = s
            vals = []
            for _ in range(PEER_TOPK + 1):
                m, cur = _extract_max(cur)
                vals.append(m)
            top.append(vals)
        v1, v2 = top
        v2_16 = jnp.concatenate(v2[:PEER_TOPK], axis=0)
        v2_8 = v2_16[:8]
        rowid = lax.broadcasted_iota(jnp.int32, v2_8.shape, 0)
        pieces = [v1[0] + v2_16, v1[1] + v2_8]
        for a in range(2, 8):
            pieces.append(jnp.where(rowid < PEER_TOPK // (a + 1), v1[a] + v2_8, NEG_INF))
        pieces.append(jnp.concatenate(v1[8:PEER_TOPK], axis=0) + v2[0])
        cur = jnp.concatenate(pieces, axis=0)
        best, cur = _extract_max(cur)
        zsum = jnp.ones_like(best)
        m = best
        for _ in range(PEER_TOPK - 1):
            m, cur = _extract_max(cur)
            zsum = zsum + jnp.exp(m - best)
        nxt = jnp.maximum(jnp.max(cur, axis=0, keepdims=True),
                          jnp.maximum(v1[PEER_TOPK] + v2[0], v1[0] + v2[PEER_TOPK]))
        tau = 0.5 * (m + nxt)
        s2 = st[1] * STEP_SCALE
        e2 = jnp.exp(st[1] - top[1][0]) / zsum
        for t in range(s2.shape[1] // LANES):
            cols = slice(t * LANES, (t + 1) * LANES)
            se_ref[h, t, :, 0] = s2[:, cols].reshape(N_KEYS // 8, 8, LANES)
            se_ref[h, t, :, 1] = e2[:, cols].reshape(N_KEYS // 8, 8, LANES)
        thr_ref[h] = (tau - st[0]) * STEP_SCALE
        e1_ref[h] = jnp.exp(st[0] - top[0][0])


def peer_select(q, keys_b, *, tp=256):
    n = q.shape[0]
    shp = jax.ShapeDtypeStruct((PEER_HEADS, N_KEYS, n), F32)
    spec = pl.BlockSpec((PEER_HEADS, N_KEYS, tp), lambda i: (0, 0, i))
    tshp = jax.ShapeDtypeStruct((PEER_HEADS, n // LANES, N_KEYS // 8, 2, 8, LANES), F32)
    tspec = pl.BlockSpec((PEER_HEADS, tp // LANES, N_KEYS // 8, 2, 8, LANES), lambda i: (0, i, 0, 0, 0, 0))
    return pl.pallas_call(
        _peer_select_kernel,
        grid=(n // tp,),
        in_specs=[
            pl.BlockSpec((tp, PEER_HEADS * D_KEY), lambda i: (i, 0)),
            pl.BlockSpec((PEER_HEADS, 2, N_KEYS, D_KEY // 2), lambda i: (0, 0, 0, 0)),
        ],
        out_specs=[tspec, spec, spec],
        out_shape=[tshp, shp, shp],
        compiler_params=_cparams(("parallel",)),
        name="peer_select",
    )(q, keys_b)


PEER_CHUNK = 256


def _peer_dense_kernel(x_ref, xnt_ref, se_ref, thr_ref, e1_ref, u_ref, vt_ref, o_ref, wt_ref, acc_ref, *, eb):
    j = pl.program_id(1)
    nchunk = eb // PEER_CHUNK
    per = PEER_CHUNK // N_KEYS

    @pl.when(j == 0)
    def _():
        acc_ref[...] = jnp.zeros(acc_ref.shape, F32)

    def scores(c):
        return jnp.dot(u_ref[c * PEER_CHUNK:(c + 1) * PEER_CHUNK, :], xnt_ref[...], preferred_element_type=F32)

    def gate_chunk(c, at):
        for ii in range(per):
            row = c * per + ii
            thr_rows = [thr_ref[h, row:row + 1, :] for h in range(PEER_HEADS)]
            e1_rows = [e1_ref[h, row:row + 1, :] for h in range(PEER_HEADS)]
            for t in range(xnt_ref.shape[1] // LANES):
                cols = slice(t * LANES, (t + 1) * LANES)
                g = None
                for h in range(PEER_HEADS):
                    s2 = se_ref[h, t, :, 0].reshape(N_KEYS, LANES)
                    e2 = se_ref[h, t, :, 1].reshape(N_KEYS, LANES)
                    term = jnp.maximum(jnp.minimum(s2 - thr_rows[h][:, cols], e2), 0.0) * e1_rows[h][:, cols]
                    g = term if g is None else g + term
                a = at[ii * N_KEYS:(ii + 1) * N_KEYS, cols]
                act = 0.5 * a * (1.0 + lax.erf(a * (2.0 ** -0.5)))
                wt_ref[row * N_KEYS:(row + 1) * N_KEYS, cols] = (g * act).astype(BF16)

    at_next = scores(0)
    for c in range(nchunk):
        at = at_next
        if c + 1 < nchunk:
            at_next = scores(c + 1)
        gate_chunk(c, at)
        rows = slice(c * PEER_CHUNK, (c + 1) * PEER_CHUNK)
        acc_ref[...] += jnp.dot(vt_ref[:, rows], wt_ref[rows, :], preferred_element_type=F32)

    @pl.when(j == pl.num_programs(1) - 1)
    def _():
        o_ref[...] = x_ref[...] + acc_ref[...].T


def peer_dense(x, xnt, se, thr, e1, u_all, vt_all, layer, *, tp=512, eb=1024):
    n, d = x.shape
    nb = N_EXPERTS // eb
    once = pl.Buffered(1)
    gspec = pl.BlockSpec((PEER_HEADS, tp // LANES, N_KEYS // 8, 2, 8, LANES), lambda i, j: (0, i, 0, 0, 0, 0),
                         pipeline_mode=once)
    rspec = pl.BlockSpec((PEER_HEADS, eb // N_KEYS, tp), lambda i, j: (0, j, i))
    return pl.pallas_call(
        functools.partial(_peer_dense_kernel, eb=eb),
        grid=(n // tp, nb),
        in_specs=[
            pl.BlockSpec((tp, d), lambda i, j: (i, 0), pipeline_mode=once),
            pl.BlockSpec((d, tp), lambda i, j: (0, i), pipeline_mode=once),
            gspec, rspec, rspec,
            pl.BlockSpec((None, eb, d), lambda i, j: (layer, j, 0)),
            pl.BlockSpec((None, d, eb), lambda i, j: (layer, 0, j)),
        ],
        out_specs=pl.BlockSpec((tp, d), lambda i, j: (i, 0)),
        out_shape=jax.ShapeDtypeStruct((n, d), F32),
        scratch_shapes=[pltpu.VMEM((eb, tp), BF16), pltpu.VMEM((d, tp), F32)],
        compiler_params=_cparams(("parallel", "arbitrary")),
        name="peer_dense",
    )(x, xnt, se, thr, e1, u_all, vt_all)


def _final_norm_kernel(x_ref, g_ref, o_ref):
    o_ref[...] = _rms(x_ref[...], g_ref[...])


def final_norm(x, g, *, tm=512):
    n, d = x.shape
    spec = pl.BlockSpec((tm, d), lambda i: (i, 0))
    return pl.pallas_call(
        _final_norm_kernel,
        grid=(n // tm,),
        in_specs=[spec, pl.BlockSpec((1, d), lambda i: (0, 0))],
        out_specs=spec,
        out_shape=jax.ShapeDtypeStruct((n, d), F32),
        compiler_params=_cparams(("parallel",)),
        name="final_norm",
    )(x, g.reshape(1, d))


def _prep_w_in(w_in):
    qa, ka, va, oa, ig, fg, glu, cq, ckv, kr = jnp.split(w_in, SPLIT_IDX, axis=-1)
    pad = jnp.zeros(w_in.shape[:-1] + (PROJ_COLS - IN_COLS,), w_in.dtype)
    return jnp.concatenate([qa, ka, va, oa, glu, cq, ckv, kr, ig, fg, pad], axis=-1).astype(BF16)


def _prep_w_uq(w_uq):
    w = w_uq.reshape(w_uq.shape[:-1] + (C_HEADS, C_NOPE + C_ROPE))
    half = C_ROPE // 2
    flat = lambda a: a.reshape(a.shape[:-2] + (-1,))
    return jnp.concatenate(
        [flat(w[..., :C_NOPE]), flat(w[..., C_NOPE:C_NOPE + half]), flat(w[..., C_NOPE + half:])],
        axis=-1).astype(BF16)


def _rope_tables(pos):
    half = C_ROPE // 2
    inv = ROPE_BASE ** (-jnp.arange(half, dtype=jnp.float32) / half)
    ang = pos.astype(jnp.float32)[:, None] * inv[None, :]
    return jnp.tile(jnp.cos(ang), (1, C_HEADS)), jnp.tile(jnp.sin(ang), (1, C_HEADS))


def kernel(x_prompt, x_sample, state_mlstm_C, state_mlstm_n, state_mlstm_m, state_conv, cache_kv_latent, cache_k_rope, norm1_g, w_in, b_gate, mlstm_norm_g, conv_w, conv_b, conv_ln_g, conv_ln_b, q_norm_g, kv_norm_g, w_uq, w_uk, w_uv, w_out, norm2_g, peer_wq, peer_keys, peer_u, peer_v, final_g):
    bp, sp, _ = x_prompt.shape
    bs, ts, _ = x_sample.shape
    past = cache_kv_latent.shape[2]
    n_p = bp * sp
    n_s = bs * ts

    x = jnp.concatenate([x_prompt.reshape(n_p, D_MODEL), x_sample.reshape(n_s, D_MODEL)], axis=0)
    pos = jnp.concatenate([jnp.tile(jnp.arange(sp), bp), jnp.tile(past + jnp.arange(ts), bs)])
    cos_t, sin_t = _rope_tables(pos)

    w_in_b = _prep_w_in(w_in)
    w_uq_b = _prep_w_uq(w_uq)
    w_uk_b = w_uk.astype(BF16)
    w_uv_b = w_uv.astype(BF16)
    w_out_b = w_out.astype(BF16)
    peer_wq_b = peer_wq.astype(BF16)
    peer_keys_b = peer_keys.astype(BF16)
    peer_u_b = peer_u.astype(BF16)
    peer_vt_b = jnp.swapaxes(peer_v.astype(BF16), 1, 2)
    gate_bias = jnp.pad(b_gate, ((0, 0), (GATE_LANE, 128 - GATE_LANE - 2 * A_HEADS)))[:, None, :]

    zc = jnp.zeros((bp, A_HEADS, A_HEAD_DIM, A_HEAD_DIM), F32)
    zn = jnp.zeros((bp, A_HEADS, A_HEAD_DIM), F32)
    zm = jnp.zeros((bp, A_HEADS), F32)
    zbuf = jnp.zeros((bp, CONV_WIDTH - 1, D_B), F32)

    p_st = [[] for _ in range(6)]
    s_st = [[] for _ in range(6)]
    for l in range(DEPTH):
        proj, _ = norm_matmul(x, norm1_g[l], w_in_b, l)
        ya_p, c_p, nn_p, m_p = mlstm_group(proj, 0, bp, sp, CHUNK, zc, zn, zm, gate_bias[l], mlstm_norm_g[l])
        ya_s, c_s, nn_s, m_s = mlstm_group(proj, n_p, bs, ts, ts, state_mlstm_C[l], state_mlstm_n[l],
                                           state_mlstm_m[l], gate_bias[l], mlstm_norm_g[l])
        yb_p, buf_p = conv_group(proj, 0, bp, sp, 256, zbuf, conv_w[l], conv_b[l], conv_ln_g[l], conv_ln_b[l])
        yb_s, buf_s = conv_group(proj, n_p, bs, ts, ts, state_conv[l], conv_w[l], conv_b[l], conv_ln_g[l],
                                 conv_ln_b[l])
        q_lat, q_rope, ckv, kr, ckv_b, kr_b = mla_prep(proj, cos_t, sin_t, q_norm_g[l], kv_norm_g[l], w_uq_b[l],
                                                       w_uk_b[l])
        yc_p = attn_prompt(q_lat, q_rope, ckv_b, kr_b, w_uv_b[l], bp, sp)
        yc_s = attn_sample(q_lat, q_rope, ckv_b, kr_b, cache_kv_latent, cache_k_rope, l, w_uv_b[l], n_p, bs, ts)
        ya = jnp.concatenate([ya_p, ya_s], axis=0)
        yb = jnp.concatenate([yb_p, yb_s], axis=0)
        yc = jnp.concatenate([yc_p, yc_s], axis=0)
        x = out_proj(x, ya, yb, yc, w_out_b, l)
        q, xnt = norm_matmul(x, norm2_g[l], peer_wq_b, l, emit_xn=True)
        se, thr, e1 = peer_select(q, peer_keys_b[l])
        x = peer_dense(x, xnt, se, thr, e1, peer_u_b, peer_vt_b, l)

        for lst, vals in ((p_st, (c_p, nn_p, m_p, buf_p, ckv[:n_p].reshape(bp, sp, KV_RANK),
                                  kr[:n_p].reshape(bp, sp, C_ROPE))),
                          (s_st, (c_s, nn_s, m_s, buf_s, ckv[n_p:].reshape(bs, ts, KV_RANK),
                                  kr[n_p:].reshape(bs, ts, C_ROPE)))):
            for i in range(6):
                lst[i].append(vals[i])

    y = final_norm(x, final_g)
    y_prompt = y[:n_p].reshape(bp, sp, D_MODEL)
    y_sample = y[n_p:].reshape(bs, ts, D_MODEL)
    return (y_prompt, y_sample, *[jnp.stack(a) for a in p_st], *[jnp.stack(a) for a in s_st])
```

```python
import functools

import numpy as np
import jax
import jax.numpy as jnp
from jax import lax
from jax.experimental import pallas as pl
from jax.experimental.pallas import tpu as pltpu

F32 = jnp.float32
BF16 = jnp.bfloat16
HIGHEST = lax.Precision.HIGHEST

D_MODEL = 2048
DEPTH = 4
CHUNK = 64
EPS = 1e-6
A_HEADS = 4
A_HEAD_DIM = 128
D_A = A_HEADS * A_HEAD_DIM
D_B = 512
CONV_WIDTH = 31
C_HEADS = 8
C_NOPE = 128
C_ROPE = 64
C_VDIM = 128
Q_RANK = 512
KV_RANK = 256
D_C = C_HEADS * C_VDIM
ROPE_BASE = 10000.0
ATTN_SCALE = (C_NOPE + C_ROPE) ** -0.5
Q_SCALE = ATTN_SCALE * float(np.log2(np.e))
SPLIT_SIZES = (D_A, D_A, D_A, D_A, A_HEADS, A_HEADS, 2 * D_B, Q_RANK, KV_RANK, C_ROPE)
SPLIT_IDX = tuple(int(i) for i in np.cumsum(SPLIT_SIZES)[:-1])
IN_COLS = int(sum(SPLIT_SIZES))
PEER_HEADS = 8
N_KEYS = 128
N_EXPERTS = N_KEYS * N_KEYS
D_KEY = 256
PEER_TOPK = 16

PROJ_COLS = 4096
GATE_LANE = C_ROPE
CB_Q, CB_K, CB_V, CB_O = 0, 1, 2, 3
CB_GLU = 2
CB_CQ = 6
CB_CKV = 14
CB_KRG = 30

V7X_VMEM_LIMIT = 60 * 1024 * 1024
NEG_INF = float("-inf")


def _cparams(sem):
    return pltpu.CompilerParams(dimension_semantics=sem, vmem_limit_bytes=V7X_VMEM_LIMIT)


def _rms(x, g):
    return x * lax.rsqrt(jnp.mean(x * x, axis=-1, keepdims=True) + EPS) * g


def _log_sigmoid(x):
    return jnp.minimum(x, 0.0) - jnp.log1p(jnp.exp(-jnp.abs(x)))


def _norm_matmul_kernel(x_ref, g_ref, w_ref, o_ref, *rest, emit_xn):
    xn_out_ref = rest[0] if emit_xn else None
    xn_ref = rest[-1]

    @pl.when(pl.program_id(1) == 0)
    def _():
        yb = _rms(x_ref[...], g_ref[...]).astype(BF16)
        xn_ref[...] = yb
        if emit_xn:
            xn_out_ref[...] = yb.T

    o_ref[...] = jnp.dot(xn_ref[...], w_ref[...], preferred_element_type=F32)


def norm_matmul(x, g, w_all, layer, *, emit_xn=False, tm=1024, tn=1024):
    n, d = x.shape
    m = w_all.shape[2]
    out_shape = [jax.ShapeDtypeStruct((n, m), F32)]
    out_specs = [pl.BlockSpec((tm, tn), lambda i, j: (i, j))]
    if emit_xn:
        out_shape.append(jax.ShapeDtypeStruct((d, n), BF16))
        out_specs.append(pl.BlockSpec((d, tm), lambda i, j: (0, i)))
    outs = pl.pallas_call(
        functools.partial(_norm_matmul_kernel, emit_xn=emit_xn),
        grid=(n // tm, m // tn),
        in_specs=[
            pl.BlockSpec((tm, d), lambda i, j: (i, 0)),
            pl.BlockSpec((1, d), lambda i, j: (0, 0)),
            pl.BlockSpec((None, d, tn), lambda i, j: (layer, 0, j)),
        ],
        out_specs=out_specs,
        out_shape=out_shape,
        scratch_shapes=[pltpu.VMEM((tm, d), BF16)],
        compiler_params=_cparams(("parallel", "arbitrary")),
        name="norm_matmul",
    )(x, g.reshape(1, d), w_all)
    return (outs[0], outs[1]) if emit_xn else (outs[0], None)


def _mlstm_kernel(*refs, groups):
    n_in, n_out = 8, 4
    bias_ref, ng_ref = refs[n_in * groups], refs[n_in * groups + 1]
    outs = refs[n_in * groups + 2:]

    @pl.when(pl.program_id(1) == 0)
    def _():
        for g in range(groups):
            c0_ref, n0_ref, m0_ref = refs[n_in * g + 5:n_in * g + 8]
            _, c_ref, n_ref, m_ref = outs[n_out * g:n_out * (g + 1)]
            c_ref[...] = c0_ref[...]
            n_ref[...] = n0_ref[...]
            m_ref[...] = m0_ref[...]

    L = refs[0].shape[0]
    row = lax.broadcasted_iota(jnp.int32, (L, L), 0)
    col = lax.broadcasted_iota(jnp.int32, (L, L), 1)
    causal = col <= row
    tril = causal.astype(F32)
    triu = (row <= col).astype(F32)
    sel = (lax.broadcasted_iota(jnp.int32, (8, 128), 1)
           == lax.broadcasted_iota(jnp.int32, (8, 128), 0) + GATE_LANE).astype(F32)

    inst = []
    for g in range(groups):
        q_ref, k_ref, v_ref, o_ref, g_ref = refs[n_in * g:n_in * g + 5]
        y_ref, c_ref, n_ref, m_ref = outs[n_out * g:n_out * (g + 1)]
        pre = g_ref[...] + bias_ref[...]
        b_all = jnp.dot(tril, _log_sigmoid(pre), precision=HIGHEST,
                        preferred_element_type=F32)
        rows = lax.dot_general(sel, pre, (((1,), (1,)), ((), ())), precision=HIGHEST,
                               preferred_element_type=F32)
        b_rows = jnp.dot(_log_sigmoid(rows[A_HEADS:2 * A_HEADS]), triu, precision=HIGHEST,
                         preferred_element_type=F32)
        for h in range(A_HEADS):
            sl = slice(h * A_HEAD_DIM, (h + 1) * A_HEAD_DIM)
            e = dict(sl=sl, h=h, y_ref=y_ref, c_ref=c_ref, n_ref=n_ref, m_ref=m_ref)
            e["q"] = q_ref[:, sl]
            e["k"] = k_ref[:, sl] * (A_HEAD_DIM ** -0.5)
            e["vb"] = v_ref[:, sl].astype(BF16)
            e["og"] = jax.nn.sigmoid(o_ref[:, sl])
            e["b_col"] = b_all[:, GATE_LANE + A_HEADS + h:GATE_LANE + A_HEADS + h + 1]
            e["ig_col"] = pre[:, GATE_LANE + h:GATE_LANE + h + 1]
            e["b_row"] = b_rows[h:h + 1]
            e["ig_row"] = rows[h:h + 1]
            e["m_prev"] = m_ref[0, h:h + 1, 0:1]
            e["c_prev"] = c_ref[0, h]
            e["n_prev"] = n_ref[0, h:h + 1, :]
            qb = e["q"].astype(BF16)
            e["qk"] = lax.dot_general(qb, e["k"].astype(BF16), (((1,), (1,)), ((), ())), preferred_element_type=F32)
            e["qc"] = jnp.dot(qb, e["c_prev"].astype(BF16), preferred_element_type=F32)
            inst.append(e)

    for e in inst:
        dm = jnp.where(causal, e["b_col"] - e["b_row"] + e["ig_row"], NEG_INF)
        inter = e["b_col"] + e["m_prev"]
        m_t = jnp.maximum(inter, jnp.max(dm, axis=1, keepdims=True))
        e["w_inter"] = jnp.exp(inter - m_t)
        e["s"] = e["qk"] * jnp.exp(dm - m_t)
        e["m_t"] = m_t
        m_new = m_t[L - 1:L, :]
        b_last = e["b_col"][L - 1:L, :]
        e["m_new"] = m_new
        e["decay"] = jnp.exp(b_last + e["m_prev"] - m_new)
        e["kw"] = e["k"] * jnp.exp(b_last - e["b_col"] + e["ig_col"] - m_new)

    for e in inst:
        e["sv"] = jnp.dot(e["s"].astype(BF16), e["vb"], preferred_element_type=F32)
        e["kv"] = lax.dot_general(e["kw"].astype(BF16), e["vb"], (((0,), (0,)), ((), ())),
                                  preferred_element_type=F32)

    for e in inst:
        h, sl = e["h"], e["sl"]
        num = e["sv"] + e["w_inter"] * e["qc"]
        den = (jnp.sum(e["s"], axis=1, keepdims=True)
               + e["w_inter"] * jnp.sum(e["q"] * e["n_prev"], axis=1, keepdims=True))
        hh = num / jnp.maximum(jnp.abs(den), jnp.exp(-e["m_t"]))
        e["c_ref"][0, h] = e["decay"] * e["c_prev"] + e["kv"]
        e["n_ref"][0, h:h + 1, :] = e["decay"] * e["n_prev"] + jnp.sum(e["kw"], axis=0, keepdims=True)
        e["m_ref"][0, h:h + 1, :] = jnp.broadcast_to(e["m_new"], (1, 128))
        e["y_ref"][:, sl] = (e["og"] * _rms(hh, ng_ref[:, sl])).astype(BF16)


def mlstm_group(proj, row0, bsz, t, chunk_len, c0, n0, m0, gate_bias, norm_g, *, groups):
    nc = t // chunk_len
    rb0 = row0 // chunk_len
    per = bsz // groups
    m0p = jnp.pad(jnp.broadcast_to(m0[:, :, None], (bsz, A_HEADS, 128)), ((0, 0), (0, 8 - A_HEADS), (0, 0)))

    in_specs, out_specs, out_shape, args = [], [], [], []
    for g in range(groups):
        def col_spec(width, cb, g=g):
            return pl.BlockSpec((chunk_len, width), lambda b, c: (rb0 + (g * per + b) * nc + c, cb))

        st4 = lambda b, c, g=g: (g * per + b, 0, 0, 0)
        st3 = lambda b, c, g=g: (g * per + b, 0, 0)
        in_specs += [
            col_spec(D_A, CB_Q), col_spec(D_A, CB_K), col_spec(D_A, CB_V), col_spec(D_A, CB_O),
            col_spec(128, CB_KRG),
            pl.BlockSpec((1, A_HEADS, A_HEAD_DIM, A_HEAD_DIM), st4),
            pl.BlockSpec((1, A_HEADS, A_HEAD_DIM), st3),
            pl.BlockSpec((1, 8, 128), st3),
        ]
        args += [proj, proj, proj, proj, proj, c0, n0, m0p]
        out_specs += [
            pl.BlockSpec((chunk_len, D_A), lambda b, c: (b * nc + c, 0)),
            pl.BlockSpec((1, A_HEADS, A_HEAD_DIM, A_HEAD_DIM), lambda b, c: (b, 0, 0, 0)),
            pl.BlockSpec((1, A_HEADS, A_HEAD_DIM), lambda b, c: (b, 0, 0)),
            pl.BlockSpec((1, 8, 128), lambda b, c: (b, 0, 0)),
        ]
        out_shape += [
            jax.ShapeDtypeStruct((per * t, D_A), BF16),
            jax.ShapeDtypeStruct((per, A_HEADS, A_HEAD_DIM, A_HEAD_DIM), F32),
            jax.ShapeDtypeStruct((per, A_HEADS, A_HEAD_DIM), F32),
            jax.ShapeDtypeStruct((per, 8, 128), F32),
        ]
    in_specs += [pl.BlockSpec((1, 128), lambda b, c: (0, 0)), pl.BlockSpec((1, D_A), lambda b, c: (0, 0))]
    args += [gate_bias, norm_g.reshape(1, D_A)]
    outs = pl.pallas_call(
        functools.partial(_mlstm_kernel, groups=groups),
        grid=(per, nc),
        in_specs=in_specs,
        out_specs=out_specs,
        out_shape=out_shape,
        compiler_params=_cparams(("parallel", "arbitrary")),
        name="mlstm",
    )(*args)
    y, c_f, n_f, m_f = (jnp.concatenate(outs[k::4], axis=0) for k in range(4))
    return y, c_f, n_f, m_f[:, :A_HEADS, 0]


CONV_HALO = 32
CONV_ROWS = 32


def _conv_kernel(glu_ref, buf_ref, w_ref, b_ref, lng_ref, lnb_ref, y_ref, nbuf_ref, xp_ref):
    tc = glu_ref.shape[0]
    j = pl.program_id(1)
    hist = CONV_WIDTH - 1

    @pl.when(j == 0)
    def _():
        xp_ref[0:CONV_HALO - hist, :] = jnp.zeros((CONV_HALO - hist, D_B), F32)
        xp_ref[CONV_HALO - hist:CONV_HALO, :] = buf_ref[0]

    @pl.when(j > 0)
    def _():
        xp_ref[0:CONV_HALO, :] = xp_ref[tc:tc + CONV_HALO, :]

    u = glu_ref[:, 0:D_B] * jax.nn.sigmoid(glu_ref[:, D_B:2 * D_B])
    xp_ref[CONV_HALO:CONV_HALO + tc, :] = u

    @pl.when(j == pl.num_programs(1) - 1)
    def _():
        nbuf_ref[0] = xp_ref[CONV_HALO + tc - hist:CONV_HALO + tc, :]

    for r0 in range(0, tc, CONV_ROWS):
        acc = jnp.broadcast_to(b_ref[...], (CONV_ROWS, D_B))
        for tap in range(CONV_WIDTH):
            start = CONV_HALO - hist + r0 + tap
            acc = acc + w_ref[tap:tap + 1, :] * xp_ref[start:start + CONV_ROWS, :]
        mu = jnp.mean(acc, axis=-1, keepdims=True)
        cen = acc - mu
        var = jnp.mean(cen * cen, axis=-1, keepdims=True)
        z = cen * lax.rsqrt(var + EPS) * lng_ref[...] + lnb_ref[...]
        y_ref[r0:r0 + CONV_ROWS, :] = (z * jax.nn.sigmoid(z)).astype(BF16)


def conv_group(proj, row0, bsz, t, tc, buf, w, b, ln_g, ln_b):
    nt = t // tc
    rb0 = row0 // tc
    const2 = lambda bb, j: (0, 0)
    y, nbuf = pl.pallas_call(
        _conv_kernel,
        grid=(bsz, nt),
        in_specs=[
            pl.BlockSpec((tc, 2 * D_B), lambda bb, j: (rb0 + bb * nt + j, CB_GLU)),
            pl.BlockSpec((1, CONV_WIDTH - 1, D_B), lambda bb, j: (bb, 0, 0)),
            pl.BlockSpec((CONV_WIDTH, D_B), const2),
            pl.BlockSpec((1, D_B), const2),
            pl.BlockSpec((1, D_B), const2),
            pl.BlockSpec((1, D_B), const2),
        ],
        out_specs=[
            pl.BlockSpec((tc, D_B), lambda bb, j: (bb * nt + j, 0)),
            pl.BlockSpec((1, CONV_WIDTH - 1, D_B), lambda bb, j: (bb, 0, 0)),
        ],
        out_shape=[
            jax.ShapeDtypeStruct((bsz * t, D_B), BF16),
            jax.ShapeDtypeStruct((bsz, CONV_WIDTH - 1, D_B), F32),
        ],
        scratch_shapes=[pltpu.VMEM((CONV_HALO + tc + CONV_HALO, D_B), F32)],
        compiler_params=_cparams(("parallel", "arbitrary")),
        name="conv_module",
    )(proj, buf, w, b.reshape(1, D_B), ln_g.reshape(1, D_B), ln_b.reshape(1, D_B))
    return y, nbuf


def _mla_prep_kernel(cq_ref, ckv_ref, krg_ref, cos_ref, sin_ref, qg_ref, kvg_ref, wuq_ref, wuk_ref,
                     qlat_ref, qrope_ref, ckv_out_ref, kr_out_ref, ckv_b_ref, kr_b_ref):
    half = C_ROPE // 2
    cq = _rms(cq_ref[...], qg_ref[...]).astype(BF16)
    qh = jnp.dot(cq, wuq_ref[...], preferred_element_type=F32)
    cos = cos_ref[...]
    sin = sin_ref[...]
    x1 = qh[:, D_C:D_C + C_HEADS * half]
    x2 = qh[:, D_C + C_HEADS * half:D_C + 2 * C_HEADS * half]
    r1 = (x1 * cos - x2 * sin) * Q_SCALE
    r2 = (x2 * cos + x1 * sin) * Q_SCALE
    for h in range(C_HEADS):
        qn = qh[:, h * C_NOPE:(h + 1) * C_NOPE].astype(BF16)
        ql = lax.dot_general(qn, wuk_ref[h], (((1,), (1,)), ((), ())), preferred_element_type=F32)
        qlat_ref[h] = (ql * Q_SCALE).astype(BF16)
        qrope_ref[h] = jnp.concatenate(
            [r1[:, h * half:(h + 1) * half], r2[:, h * half:(h + 1) * half]], axis=1).astype(BF16)
    ckv = _rms(ckv_ref[...], kvg_ref[...])
    ckv_out_ref[...] = ckv
    ckv_b_ref[...] = ckv.astype(BF16)
    k1 = krg_ref[:, 0:half]
    k2 = krg_ref[:, half:C_ROPE]
    c1 = cos[:, 0:half]
    s1 = sin[:, 0:half]
    kr = jnp.concatenate([k1 * c1 - k2 * s1, k2 * c1 + k1 * s1], axis=1)
    kr_out_ref[...] = kr
    kr_b_ref[...] = kr.astype(BF16)


def mla_prep(proj, cos_t, sin_t, q_norm_g, kv_norm_g, w_uq_r, w_uk_b, *, tq=256):
    n = proj.shape[0]
    const2 = lambda i: (0, 0)
    return pl.pallas_call(
        _mla_prep_kernel,
        grid=(n // tq,),
        in_specs=[
            pl.BlockSpec((tq, Q_RANK), lambda i: (i, CB_CQ)),
            pl.BlockSpec((tq, KV_RANK), lambda i: (i, CB_CKV)),
            pl.BlockSpec((tq, 128), lambda i: (i, CB_KRG)),
            pl.BlockSpec((tq, C_HEADS * C_ROPE // 2), lambda i: (i, 0)),
            pl.BlockSpec((tq, C_HEADS * C_ROPE // 2), lambda i: (i, 0)),
            pl.BlockSpec((1, Q_RANK), const2),
            pl.BlockSpec((1, KV_RANK), const2),
            pl.BlockSpec((Q_RANK, C_HEADS * (C_NOPE + C_ROPE)), const2),
            pl.BlockSpec((C_HEADS, KV_RANK, C_NOPE), lambda i: (0, 0, 0)),
        ],
        out_specs=[
            pl.BlockSpec((C_HEADS, tq, KV_RANK), lambda i: (0, i, 0)),
            pl.BlockSpec((C_HEADS, tq, C_ROPE), lambda i: (0, i, 0)),
            pl.BlockSpec((tq, KV_RANK), lambda i: (i, 0)),
            pl.BlockSpec((tq, C_ROPE), lambda i: (i, 0)),
            pl.BlockSpec((tq, KV_RANK), lambda i: (i, 0)),
            pl.BlockSpec((tq, C_ROPE), lambda i: (i, 0)),
        ],
        out_shape=[
            jax.ShapeDtypeStruct((C_HEADS, n, KV_RANK), BF16),
            jax.ShapeDtypeStruct((C_HEADS, n, C_ROPE), BF16),
            jax.ShapeDtypeStruct((n, KV_RANK), F32),
            jax.ShapeDtypeStruct((n, C_ROPE), F32),
            jax.ShapeDtypeStruct((n, KV_RANK), BF16),
            jax.ShapeDtypeStruct((n, C_ROPE), BF16),
        ],
        compiler_params=_cparams(("parallel",)),
        name="mla_prep",
    )(proj, proj, proj, cos_t, sin_t, q_norm_g.reshape(1, Q_RANK), kv_norm_g.reshape(1, KV_RANK), w_uq_r, w_uk_b)


ATT_SLAB = 256
LANES = 128
_NT = (((1,), (1,)), ((), ()))


ATT_GROUP = 8


def _attn_slabs(slabs, kcb, krb, bias, m_ref, l_ref, acc_ref):
    for g0 in range(0, len(slabs), ATT_GROUP):
        group = slabs[g0:g0 + ATT_GROUP]
        work = []
        for q, qr, rows in group:
            s = (lax.dot_general(q(), kcb, _NT, preferred_element_type=F32)
                 + lax.dot_general(qr(), krb, _NT, preferred_element_type=F32))
            work.append(dict(rows=rows, s=s))
        for w in work:
            s = w["s"] if bias is None else w["s"] + bias
            tiles = [s[:, t * LANES:(t + 1) * LANES] for t in range(s.shape[1] // LANES)]
            m_prev = m_ref[w["rows"], :]
            m_new = jnp.maximum(m_prev, jnp.max(functools.reduce(jnp.maximum, tiles), axis=1, keepdims=True))
            w["alpha"] = jnp.exp2(m_prev - m_new)
            ps = [jnp.exp2(t - m_new) for t in tiles]
            w["m_new"] = m_new
            w["l_new"] = w["alpha"] * l_ref[w["rows"], :] + functools.reduce(jnp.add, ps)
            w["p"] = jnp.concatenate(ps, axis=1).astype(BF16)
        for w in work:
            w["pv"] = jnp.dot(w["p"], kcb, preferred_element_type=F32)
        for w in work:
            rows = w["rows"]
            acc_ref[rows, :] = jnp.concatenate([w["alpha"]] * (KV_RANK // LANES), axis=1) * acc_ref[rows, :] + w["pv"]
            l_ref[rows, :] = w["l_new"]
            m_ref[rows, :] = w["m_new"]


def _attn_finish(acc_ref, l_ref, wuv_ref, y_ref, tq):
    for h in range(C_HEADS):
        rows = slice(h * tq, (h + 1) * tq)
        o = acc_ref[rows, :] / jnp.sum(l_ref[rows, :], axis=1, keepdims=True)
        y_ref[:, h * C_VDIM:(h + 1) * C_VDIM] = jnp.dot(
            o.astype(BF16), wuv_ref[h], preferred_element_type=F32).astype(BF16)


def _attn_init(m_ref, l_ref, acc_ref):
    m_ref[...] = jnp.full(m_ref.shape, NEG_INF, F32)
    l_ref[...] = jnp.zeros(l_ref.shape, F32)
    acc_ref[...] = jnp.zeros(acc_ref.shape, F32)


def _attn_prompt_kernel(ql_ref, qr_ref, k_ref, kr_ref, wuv_ref, y_ref, m_ref, l_ref, acc_ref, *, kb):
    tq = ql_ref.shape[1]
    i = pl.program_id(1)
    _attn_init(m_ref, l_ref, acc_ref)

    def block(start, bias):
        kcb = k_ref[pl.ds(start, kb), :]
        krb = kr_ref[pl.ds(start, kb), :]
        slabs = [(functools.partial(lambda h: ql_ref[h], h), functools.partial(lambda h: qr_ref[h], h),
                  slice(h * tq, (h + 1) * tq)) for h in range(C_HEADS)]
        _attn_slabs(slabs, kcb, krb, bias, m_ref, l_ref, acc_ref)

    diag = (i * tq) // kb

    def body(j, carry):
        block(pl.multiple_of(j * kb, kb), None)
        return carry

    lax.fori_loop(0, diag, body, 0)
    start = pl.multiple_of(diag * kb, kb)
    qchunk = (i * tq + lax.broadcasted_iota(jnp.int32, (tq, kb), 0)) // CHUNK
    kchunk = (start + lax.broadcasted_iota(jnp.int32, (tq, kb), 1)) // CHUNK
    block(start, jnp.where(kchunk <= qchunk, 0.0, NEG_INF))
    _attn_finish(acc_ref, l_ref, wuv_ref, y_ref, tq)


def attn_prompt(q_lat, q_rope, ckv_b, kr_b, w_uv_b, bsz, t, *, tq=ATT_SLAB, kb=512):
    nq = t // tq
    rows = C_HEADS * tq
    return pl.pallas_call(
        functools.partial(_attn_prompt_kernel, kb=kb),
        grid=(bsz, nq),
        in_specs=[
            pl.BlockSpec((C_HEADS, tq, KV_RANK), lambda b, i: (0, b * nq + i, 0)),
            pl.BlockSpec((C_HEADS, tq, C_ROPE), lambda b, i: (0, b * nq + i, 0)),
            pl.BlockSpec((t, KV_RANK), lambda b, i: (b, 0)),
            pl.BlockSpec((t, C_ROPE), lambda b, i: (b, 0)),
            pl.BlockSpec((C_HEADS, KV_RANK, C_VDIM), lambda b, i: (0, 0, 0)),
        ],
        out_specs=pl.BlockSpec((tq, D_C), lambda b, i: (b * nq + i, 0)),
        out_shape=jax.ShapeDtypeStruct((bsz * t, D_C), BF16),
        scratch_shapes=[pltpu.VMEM((rows, LANES), F32), pltpu.VMEM((rows, LANES), F32),
                        pltpu.VMEM((rows, KV_RANK), F32)],
        compiler_params=_cparams(("parallel", "arbitrary")),
        name="attn_prompt",
    )(q_lat, q_rope, ckv_b, kr_b, w_uv_b)


def _attn_sample_kernel(ql_ref, qr_ref, k_ref, kr_ref, ck_ref, ckr_ref, wuv_ref, y_ref, m_ref, l_ref, acc_ref,
                        *, kb):
    tq = ql_ref.shape[1]
    rows_total = C_HEADS * tq
    hps = ATT_SLAB // tq
    _attn_init(m_ref, l_ref, acc_ref)
    past = ck_ref.shape[1]

    def block(kcb, krb, bias):
        slabs = [(functools.partial(lambda i: ql_ref[i * hps:(i + 1) * hps].reshape(ATT_SLAB, KV_RANK), sidx),
                  functools.partial(lambda i: qr_ref[i * hps:(i + 1) * hps].reshape(ATT_SLAB, C_ROPE), sidx),
                  slice(sidx * ATT_SLAB, (sidx + 1) * ATT_SLAB)) for sidx in range(rows_total // ATT_SLAB)]
        _attn_slabs(slabs, kcb, krb, bias, m_ref, l_ref, acc_ref)

    def body(j, carry):
        start = pl.multiple_of(j * kb, kb)
        block(ck_ref[0, pl.ds(start, kb), :].astype(BF16), ckr_ref[0, pl.ds(start, kb), :].astype(BF16), None)
        return carry

    lax.fori_loop(0, past // kb, body, 0)
    pad = LANES - tq
    kcb = jnp.concatenate([k_ref[...], jnp.zeros((pad, KV_RANK), BF16)], axis=0)
    krb = jnp.concatenate([kr_ref[...], jnp.zeros((pad, C_ROPE), BF16)], axis=0)
    bias = jnp.where(lax.broadcasted_iota(jnp.int32, (1, LANES), 1) < tq, 0.0, NEG_INF)
    block(kcb, krb, bias)
    _attn_finish(acc_ref, l_ref, wuv_ref, y_ref, tq)


def attn_sample(q_lat, q_rope, ckv_b, kr_b, cache_lat, cache_rope, layer, w_uv_b, row0, bsz, t, *, kb=1024):
    rb0 = row0 // t
    past = cache_lat.shape[2]
    rows = C_HEADS * t
    return pl.pallas_call(
        functools.partial(_attn_sample_kernel, kb=kb),
        grid=(bsz,),
        in_specs=[
            pl.BlockSpec((C_HEADS, t, KV_RANK), lambda b: (0, rb0 + b, 0)),
            pl.BlockSpec((C_HEADS, t, C_ROPE), lambda b: (0, rb0 + b, 0)),
            pl.BlockSpec((t, KV_RANK), lambda b: (rb0 + b, 0)),
            pl.BlockSpec((t, C_ROPE), lambda b: (rb0 + b, 0)),
            pl.BlockSpec((None, 1, past, KV_RANK), lambda b: (layer, b, 0, 0)),
            pl.BlockSpec((None, 1, past, C_ROPE), lambda b: (layer, b, 0, 0)),
            pl.BlockSpec((C_HEADS, KV_RANK, C_VDIM), lambda b: (0, 0, 0)),
        ],
        out_specs=pl.BlockSpec((t, D_C), lambda b: (b, 0)),
        out_shape=jax.ShapeDtypeStruct((bsz * t, D_C), BF16),
        scratch_shapes=[pltpu.VMEM((rows, LANES), F32), pltpu.VMEM((rows, LANES), F32),
                        pltpu.VMEM((rows, KV_RANK), F32)],
        compiler_params=_cparams(("parallel",)),
        name="attn_sample",
    )(q_lat, q_rope, ckv_b, kr_b, cache_lat, cache_rope, w_uv_b)


def _oproj_kernel(x_ref, ya_ref, yb_ref, yc_ref, wa_ref, wb_ref, wc_ref, o_ref):
    acc = x_ref[...]
    acc = acc + jnp.dot(ya_ref[...], wa_ref[...], preferred_element_type=F32)
    acc = acc + jnp.dot(yb_ref[...], wb_ref[...], preferred_element_type=F32)
    acc = acc + jnp.dot(yc_ref[...], wc_ref[...], preferred_element_type=F32)
    o_ref[...] = acc


def out_proj(x, ya, yb, yc, w_out_all, layer, *, tm=1024, tn=512):
    n, d = x.shape
    return pl.pallas_call(
        _oproj_kernel,
        grid=(n // tm, d // tn),
        in_specs=[
            pl.BlockSpec((tm, tn), lambda i, j: (i, j)),
            pl.BlockSpec((tm, D_A), lambda i, j: (i, 0)),
            pl.BlockSpec((tm, D_B), lambda i, j: (i, 0)),
            pl.BlockSpec((tm, D_C), lambda i, j: (i, 0)),
            pl.BlockSpec((None, D_A, tn), lambda i, j: (layer, 0, j)),
            pl.BlockSpec((None, D_B, tn), lambda i, j: (layer, 1, j)),
            pl.BlockSpec((None, D_C, tn), lambda i, j: (layer, 1, j)),
        ],
        out_specs=pl.BlockSpec((tm, tn), lambda i, j: (i, j)),
        out_shape=jax.ShapeDtypeStruct((n, d), F32),
        compiler_params=_cparams(("parallel", "arbitrary")),
        name="out_proj",
    )(x, ya, yb, yc, w_out_all, w_out_all, w_out_all)


STEP_SCALE = 2.0 ** 64


def _extract_max(cur):
    m = jnp.max(cur, axis=0, keepdims=True)
    return m, jnp.where(cur == m, NEG_INF, cur)


def _peer_select_kernel(q_ref, keys_ref, se_ref, thr_ref, e1_ref):
    for h in range(PEER_HEADS):
        st = []
        top = []
        for z in range(2):
            qhz = q_ref[:, (2 * h + z) * N_KEYS:(2 * h + z + 1) * N_KEYS].astype(BF16)
            s = lax.dot_general(keys_ref[h, z], qhz, (((1,), (1,)), ((), ())),
                                preferred_element_type=F32)
            st.append(s)
            cur = s
            vals = []
            for _ in range(PEER_TOPK + 1):
                m, cur = _extract_max(cur)
                vals.append(m)
            top.append(vals)
        v1, v2 = top
        v2_16 = jnp.concatenate(v2[:PEER_TOPK], axis=0)
        v2_8 = v2_16[:8]
        rowid = lax.broadcasted_iota(jnp.int32, v2_8.shape, 0)
        pieces = [v1[0] + v2_16, v1[1] + v2_8]
        for a in range(2, 8):
            pieces.append(jnp.where(rowid < PEER_TOPK // (a + 1), v1[a] + v2_8, NEG_INF))
        pieces.append(jnp.concatenate(v1[8:PEER_TOPK], axis=0) + v2[0])
        cur = jnp.concatenate(pieces, axis=0)
        best, cur = _extract_max(cur)
        zsum = jnp.ones_like(best)
        m = best
        for _ in range(PEER_TOPK - 1):
            m, cur = _extract_max(cur)
            zsum = zsum + jnp.exp(m - best)
        nxt = jnp.maximum(jnp.max(cur, axis=0, keepdims=True),
                          jnp.maximum(v1[PEER_TOPK] + v2[0], v1[0] + v2[PEER_TOPK]))
        tau = 0.5 * (m + nxt)
        s2 = st[1] * STEP_SCALE
        e2 = jnp.exp(st[1] - top[1][0]) / zsum
        for t in range(s2.shape[1] // LANES):
            cols = slice(t * LANES, (t + 1) * LANES)
            se_ref[h, t, :, 0] = s2[:, cols].reshape(N_KEYS // 8, 8, LANES)
            se_ref[h, t, :, 1] = e2[:, cols].reshape(N_KEYS // 8, 8, LANES)
        thr_ref[h] = (tau - st[0]) * STEP_SCALE
        e1_ref[h] = jnp.exp(st[0] - top[0][0])


def peer_select(q, keys_b, *, tp=256):
    n = q.shape[0]
    shp = jax.ShapeDtypeStruct((PEER_HEADS, N_KEYS, n), F32)
    spec = pl.BlockSpec((PEER_HEADS, N_KEYS, tp), lambda i: (0, 0, i))
    tshp = jax.ShapeDtypeStruct((PEER_HEADS, n // LANES, N_KEYS // 8, 2, 8, LANES), F32)
    tspec = pl.BlockSpec((PEER_HEADS, tp // LANES, N_KEYS // 8, 2, 8, LANES), lambda i: (0, i, 0, 0, 0, 0))
    return pl.pallas_call(
        _peer_select_kernel,
        grid=(n // tp,),
        in_specs=[
            pl.BlockSpec((tp, PEER_HEADS * D_KEY), lambda i: (i, 0)),
            pl.BlockSpec((PEER_HEADS, 2, N_KEYS, D_KEY // 2), lambda i: (0, 0, 0, 0)),
        ],
        out_specs=[tspec, spec, spec],
        out_shape=[tshp, shp, shp],
        compiler_params=_cparams(("parallel",)),
        name="peer_select",
    )(q, keys_b)


PEER_CHUNK = 256


def _peer_dense_kernel(x_ref, xnt_ref, se_ref, thr_ref, e1_ref, u_ref, vt_ref, o_ref, wt_ref, acc_ref, *, eb):
    j = pl.program_id(1)
    nchunk = eb // PEER_CHUNK
    per = PEER_CHUNK // N_KEYS

    @pl.when(j == 0)
    def _():
        acc_ref[...] = jnp.zeros(acc_ref.shape, F32)

    def scores(c):
        return jnp.dot(u_ref[c * PEER_CHUNK:(c + 1) * PEER_CHUNK, :], xnt_ref[...], preferred_element_type=F32)

    def gate_chunk(c, at):
        for ii in range(per):
            row = c * per + ii
            thr_rows = [thr_ref[h, row:row + 1, :] for h in range(PEER_HEADS)]
            e1_rows = [e1_ref[h, row:row + 1, :] for h in range(PEER_HEADS)]
            for t in range(xnt_ref.shape[1] // LANES):
                cols = slice(t * LANES, (t + 1) * LANES)
                g = None
                for h in range(PEER_HEADS):
                    s2 = se_ref[h, t, :, 0].reshape(N_KEYS, LANES)
                    e2 = se_ref[h, t, :, 1].reshape(N_KEYS, LANES)
                    term = jnp.maximum(jnp.minimum(s2 - thr_rows[h][:, cols], e2), 0.0) * e1_rows[h][:, cols]
                    g = term if g is None else g + term
                a = at[ii * N_KEYS:(ii + 1) * N_KEYS, cols]
                act = 0.5 * a * (1.0 + lax.erf(a * (2.0 ** -0.5)))
                wt_ref[row * N_KEYS:(row + 1) * N_KEYS, cols] = (g * act).astype(BF16)

    at_next = scores(0)
    for c in range(nchunk):
        at = at_next
        if c + 1 < nchunk:
            at_next = scores(c + 1)
        gate_chunk(c, at)
        rows = slice(c * PEER_CHUNK, (c + 1) * PEER_CHUNK)
        acc_ref[...] += jnp.dot(vt_ref[:, rows], wt_ref[rows, :], preferred_element_type=F32)

    @pl.when(j == pl.num_programs(1) - 1)
    def _():
        o_ref[...] = x_ref[...] + acc_ref[...].T


def peer_dense(x, xnt, se, thr, e1, u_all, vt_all, layer, *, tp=512, eb=1024):
    n, d = x.shape
    nb = N_EXPERTS // eb
    once = pl.Buffered(1)
    gspec = pl.BlockSpec((PEER_HEADS, tp // LANES, N_KEYS // 8, 2, 8, LANES), lambda i, j: (0, i, 0, 0, 0, 0),
                         pipeline_mode=once)
    rspec = pl.BlockSpec((PEER_HEADS, eb // N_KEYS, tp), lambda i, j: (0, j, i))
    return pl.pallas_call(
        functools.partial(_peer_dense_kernel, eb=eb),
        grid=(n // tp, nb),
        in_specs=[
            pl.BlockSpec((tp, d), lambda i, j: (i, 0), pipeline_mode=once),
            pl.BlockSpec((d, tp), lambda i, j: (0, i), pipeline_mode=once),
            gspec, rspec, rspec,
            pl.BlockSpec((None, eb, d), lambda i, j: (layer, j, 0)),
            pl.BlockSpec((None, d, eb), lambda i, j: (layer, 0, j)),
        ],
        out_specs=pl.BlockSpec((tp, d), lambda i, j: (i, 0)),
        out_shape=jax.ShapeDtypeStruct((n, d), F32),
        scratch_shapes=[pltpu.VMEM((eb, tp), BF16), pltpu.VMEM((d, tp), F32)],
        compiler_params=_cparams(("parallel", "arbitrary")),
        name="peer_dense",
    )(x, xnt, se, thr, e1, u_all, vt_all)


def _final_norm_kernel(x_ref, g_ref, o_ref):
    o_ref[...] = _rms(x_ref[...], g_ref[...])


def final_norm(x, g, *, tm=512):
    n, d = x.shape
    spec = pl.BlockSpec((tm, d), lambda i: (i, 0))
    return pl.pallas_call(
        _final_norm_kernel,
        grid=(n // tm,),
        in_specs=[spec, pl.BlockSpec((1, d), lambda i: (0, 0))],
        out_specs=spec,
        out_shape=jax.ShapeDtypeStruct((n, d), F32),
        compiler_params=_cparams(("parallel",)),
        name="final_norm",
    )(x, g.reshape(1, d))


def _prep_w_in(w_in):
    qa, ka, va, oa, ig, fg, glu, cq, ckv, kr = jnp.split(w_in, SPLIT_IDX, axis=-1)
    pad = jnp.zeros(w_in.shape[:-1] + (PROJ_COLS - IN_COLS,), w_in.dtype)
    return jnp.concatenate([qa, ka, va, oa, glu, cq, ckv, kr, ig, fg, pad], axis=-1).astype(BF16)


def _prep_w_uq(w_uq):
    w = w_uq.reshape(w_uq.shape[:-1] + (C_HEADS, C_NOPE + C_ROPE))
    half = C_ROPE // 2
    flat = lambda a: a.reshape(a.shape[:-2] + (-1,))
    return jnp.concatenate(
        [flat(w[..., :C_NOPE]), flat(w[..., C_NOPE:C_NOPE + half]), flat(w[..., C_NOPE + half:])],
        axis=-1).astype(BF16)


def _stream_groups(bsz, want):
    return max(g for g in range(1, want + 1) if bsz % g == 0)


def _rope_tables(pos):
    half = C_ROPE // 2
    inv = ROPE_BASE ** (-jnp.arange(half, dtype=jnp.float32) / half)
    ang = pos.astype(jnp.float32)[:, None] * inv[None, :]
    return jnp.tile(jnp.cos(ang), (1, C_HEADS)), jnp.tile(jnp.sin(ang), (1, C_HEADS))


def kernel(x_prompt, x_sample, state_mlstm_C, state_mlstm_n, state_mlstm_m, state_conv, cache_kv_latent, cache_k_rope, norm1_g, w_in, b_gate, mlstm_norm_g, conv_w, conv_b, conv_ln_g, conv_ln_b, q_norm_g, kv_norm_g, w_uq, w_uk, w_uv, w_out, norm2_g, peer_wq, peer_keys, peer_u, peer_v, final_g):
    bp, sp, _ = x_prompt.shape
    bs, ts, _ = x_sample.shape
    past = cache_kv_latent.shape[2]
    n_p = bp * sp
    n_s = bs * ts

    x = jnp.concatenate([x_prompt.reshape(n_p, D_MODEL), x_sample.reshape(n_s, D_MODEL)], axis=0)
    pos = jnp.concatenate([jnp.tile(jnp.arange(sp), bp), jnp.tile(past + jnp.arange(ts), bs)])
    cos_t, sin_t = _rope_tables(pos)

    w_in_b = _prep_w_in(w_in)
    w_uq_b = _prep_w_uq(w_uq)
    w_uk_b = w_uk.astype(BF16)
    w_uv_b = w_uv.astype(BF16)
    w_out_b = w_out.astype(BF16)
    peer_wq_b = peer_wq.astype(BF16)
    peer_keys_b = peer_keys.astype(BF16)
    peer_u_b = peer_u.astype(BF16)
    peer_vt_b = jnp.swapaxes(peer_v.astype(BF16), 1, 2)
    gate_bias = jnp.pad(b_gate, ((0, 0), (GATE_LANE, 128 - GATE_LANE - 2 * A_HEADS)))[:, None, :]

    zc = jnp.zeros((bp, A_HEADS, A_HEAD_DIM, A_HEAD_DIM), F32)
    zn = jnp.zeros((bp, A_HEADS, A_HEAD_DIM), F32)
    zm = jnp.zeros((bp, A_HEADS), F32)
    zbuf = jnp.zeros((bp, CONV_WIDTH - 1, D_B), F32)

    p_st = [[] for _ in range(6)]
    s_st = [[] for _ in range(6)]
    for l in range(DEPTH):
        proj, _ = norm_matmul(x, norm1_g[l], w_in_b, l)
        ya_p, c_p, nn_p, m_p = mlstm_group(proj, 0, bp, sp, CHUNK, zc, zn, zm, gate_bias[l], mlstm_norm_g[l],
                                           groups=_stream_groups(bp, 2))
        ya_s, c_s, nn_s, m_s = mlstm_group(proj, n_p, bs, ts, ts, state_mlstm_C[l], state_mlstm_n[l],
                                           state_mlstm_m[l], gate_bias[l], mlstm_norm_g[l],
                                           groups=_stream_groups(bs, 4))
        yb_p, buf_p = conv_group(proj, 0, bp, sp, 256, zbuf, conv_w[l], conv_b[l], conv_ln_g[l], conv_ln_b[l])
        yb_s, buf_s = conv_group(proj, n_p, bs, ts, ts, state_conv[l], conv_w[l], conv_b[l], conv_ln_g[l],
                                 conv_ln_b[l])
        q_lat, q_rope, ckv, kr, ckv_b, kr_b = mla_prep(proj, cos_t, sin_t, q_norm_g[l], kv_norm_g[l], w_uq_b[l],
                                                       w_uk_b[l])
        yc_p = attn_prompt(q_lat, q_rope, ckv_b, kr_b, w_uv_b[l], bp, sp)
        yc_s = attn_sample(q_lat, q_rope, ckv_b, kr_b, cache_kv_latent, cache_k_rope, l, w_uv_b[l], n_p, bs, ts)
        ya = jnp.concatenate([ya_p, ya_s], axis=0)
        yb = jnp.concatenate([yb_p, yb_s], axis=0)
        yc = jnp.concatenate([yc_p, yc_s], axis=0)
        x = out_proj(x, ya, yb, yc, w_out_b, l)
        q, xnt = norm_matmul(x, norm2_g[l], peer_wq_b, l, emit_xn=True)
        se, thr, e1 = peer_select(q, peer_keys_b[l])
        x = peer_dense(x, xnt, se, thr, e1, peer_u_b, peer_vt_b, l)

        for lst, vals in ((p_st, (c_p, nn_p, m_p, buf_p, ckv[:n_p].reshape(bp, sp, KV_RANK),
                                  kr[:n_p].reshape(bp, sp, C_ROPE))),
                          (s_st, (c_s, nn_s, m_s, buf_s, ckv[n_p:].reshape(bs, ts, KV_RANK),
                                  kr[n_p:].reshape(bs, ts, C_ROPE)))):
            for i in range(6):
                lst[i].append(vals[i])

    y = final_norm(x, final_g)
    y_prompt = y[:n_p].reshape(bp, sp, D_MODEL)
    y_sample = y[n_p:].reshape(bs, ts, D_MODEL)
    return (y_prompt, y_sample, *[jnp.stack(a) for a in p_st], *[jnp.stack(a) for a in s_st])
```

```python
import functools

import numpy as np
import jax
import jax.numpy as jnp
from jax import lax
from jax.experimental import pallas as pl
from jax.experimental.pallas import tpu as pltpu

F32 = jnp.float32
BF16 = jnp.bfloat16
HIGHEST = lax.Precision.HIGHEST

D_MODEL = 2048
DEPTH = 4
CHUNK = 64
EPS = 1e-6
A_HEADS = 4
A_HEAD_DIM = 128
D_A = A_HEADS * A_HEAD_DIM
D_B = 512
CONV_WIDTH = 31
C_HEADS = 8
C_NOPE = 128
C_ROPE = 64
C_VDIM = 128
Q_RANK = 512
KV_RANK = 256
D_C = C_HEADS * C_VDIM
ROPE_BASE = 10000.0
ATTN_SCALE = (C_NOPE + C_ROPE) ** -0.5
Q_SCALE = ATTN_SCALE * float(np.log2(np.e))
SPLIT_SIZES = (D_A, D_A, D_A, D_A, A_HEADS, A_HEADS, 2 * D_B, Q_RANK, KV_RANK, C_ROPE)
SPLIT_IDX = tuple(int(i) for i in np.cumsum(SPLIT_SIZES)[:-1])
IN_COLS = int(sum(SPLIT_SIZES))
PEER_HEADS = 8
N_KEYS = 128
N_EXPERTS = N_KEYS * N_KEYS
D_KEY = 256
PEER_TOPK = 16

PROJ_COLS = 4096
GATE_LANE = C_ROPE
CB_Q, CB_K, CB_V, CB_O = 0, 1, 2, 3
CB_GLU = 2
CB_CQ = 6
CB_CKV = 14
CB_KRG = 30

V7X_VMEM_LIMIT = 60 * 1024 * 1024
NEG_INF = float("-inf")


def _cparams(sem):
    return pltpu.CompilerParams(dimension_semantics=sem, vmem_limit_bytes=V7X_VMEM_LIMIT)


def _rms(x, g):
    return x * lax.rsqrt(jnp.mean(x * x, axis=-1, keepdims=True) + EPS) * g


def _log_sigmoid(x):
    return jnp.minimum(x, 0.0) - jnp.log1p(jnp.exp(-jnp.abs(x)))


def _norm_matmul_kernel(x_ref, g_ref, w_ref, o_ref, *rest, emit_xn):
    xn_out_ref = rest[0] if emit_xn else None
    xn_ref = rest[-1]

    @pl.when(pl.program_id(1) == 0)
    def _():
        yb = _rms(x_ref[...], g_ref[...]).astype(BF16)
        xn_ref[...] = yb
        if emit_xn:
            xn_out_ref[...] = yb.T

    o_ref[...] = jnp.dot(xn_ref[...], w_ref[...], preferred_element_type=F32)


def norm_matmul(x, g, w_all, layer, *, emit_xn=False, tm=1024, tn=1024):
    n, d = x.shape
    m = w_all.shape[2]
    out_shape = [jax.ShapeDtypeStruct((n, m), F32)]
    out_specs = [pl.BlockSpec((tm, tn), lambda i, j: (i, j))]
    if emit_xn:
        out_shape.append(jax.ShapeDtypeStruct((d, n), BF16))
        out_specs.append(pl.BlockSpec((d, tm), lambda i, j: (0, i)))
    outs = pl.pallas_call(
        functools.partial(_norm_matmul_kernel, emit_xn=emit_xn),
        grid=(n // tm, m // tn),
        in_specs=[
            pl.BlockSpec((tm, d), lambda i, j: (i, 0)),
            pl.BlockSpec((1, d), lambda i, j: (0, 0)),
            pl.BlockSpec((None, d, tn), lambda i, j: (layer, 0, j)),
        ],
        out_specs=out_specs,
        out_shape=out_shape,
        scratch_shapes=[pltpu.VMEM((tm, d), BF16)],
        compiler_params=_cparams(("parallel", "arbitrary")),
        name="norm_matmul",
    )(x, g.reshape(1, d), w_all)
    return (outs[0], outs[1]) if emit_xn else (outs[0], None)


def _mlstm_kernel(*refs, groups):
    n_in, n_out = 8, 4
    bias_ref, ng_ref = refs[n_in * groups], refs[n_in * groups + 1]
    outs = refs[n_in * groups + 2:]

    @pl.when(pl.program_id(1) == 0)
    def _():
        for g in range(groups):
            c0_ref, n0_ref, m0_ref = refs[n_in * g + 5:n_in * g + 8]
            _, c_ref, n_ref, m_ref = outs[n_out * g:n_out * (g + 1)]
            c_ref[...] = c0_ref[...]
            n_ref[...] = n0_ref[...]
            m_ref[...] = m0_ref[...]

    L = refs[0].shape[0]
    row = lax.broadcasted_iota(jnp.int32, (L, L), 0)
    col = lax.broadcasted_iota(jnp.int32, (L, L), 1)
    causal = col <= row
    tril = causal.astype(F32)
    triu = (row <= col).astype(F32)
    sel = (lax.broadcasted_iota(jnp.int32, (8, 128), 1)
           == lax.broadcasted_iota(jnp.int32, (8, 128), 0) + GATE_LANE).astype(F32)

    inst = []
    for g in range(groups):
        q_ref, k_ref, v_ref, o_ref, g_ref = refs[n_in * g:n_in * g + 5]
        y_ref, c_ref, n_ref, m_ref = outs[n_out * g:n_out * (g + 1)]
        pre = g_ref[...] + bias_ref[...]
        b_all = jnp.dot(tril, _log_sigmoid(pre), precision=HIGHEST,
                        preferred_element_type=F32)
        rows = lax.dot_general(sel, pre, (((1,), (1,)), ((), ())), precision=HIGHEST,
                               preferred_element_type=F32)
        b_rows = jnp.dot(_log_sigmoid(rows[A_HEADS:2 * A_HEADS]), triu, precision=HIGHEST,
                         preferred_element_type=F32)
        for h in range(A_HEADS):
            sl = slice(h * A_HEAD_DIM, (h + 1) * A_HEAD_DIM)
            e = dict(sl=sl, h=h, y_ref=y_ref, c_ref=c_ref, n_ref=n_ref, m_ref=m_ref)
            e["q"] = q_ref[:, sl]
            e["k"] = k_ref[:, sl] * (A_HEAD_DIM ** -0.5)
            e["vb"] = v_ref[:, sl].astype(BF16)
            e["og"] = jax.nn.sigmoid(o_ref[:, sl])
            e["b_col"] = b_all[:, GATE_LANE + A_HEADS + h:GATE_LANE + A_HEADS + h + 1]
            e["ig_col"] = pre[:, GATE_LANE + h:GATE_LANE + h + 1]
            e["b_row"] = b_rows[h:h + 1]
            e["ig_row"] = rows[h:h + 1]
            e["m_prev"] = m_ref[0, h:h + 1, 0:1]
            e["c_prev"] = c_ref[0, h]
            e["n_prev"] = n_ref[0, h:h + 1, :]
            qb = e["q"].astype(BF16)
            e["qk"] = lax.dot_general(qb, e["k"].astype(BF16), (((1,), (1,)), ((), ())), preferred_element_type=F32)
            e["qc"] = jnp.dot(qb, e["c_prev"].astype(BF16), preferred_element_type=F32)
            inst.append(e)

    for e in inst:
        dm = jnp.where(causal, e["b_col"] - e["b_row"] + e["ig_row"], NEG_INF)
        inter = e["b_col"] + e["m_prev"]
        m_t = jnp.maximum(inter, jnp.max(dm, axis=1, keepdims=True))
        e["w_inter"] = jnp.exp(inter - m_t)
        e["s"] = e["qk"] * jnp.exp(dm - m_t)
        e["m_t"] = m_t
        m_new = m_t[L - 1:L, :]
        b_last = e["b_col"][L - 1:L, :]
        e["m_new"] = m_new
        e["decay"] = jnp.exp(b_last + e["m_prev"] - m_new)
        e["kw"] = e["k"] * jnp.exp(b_last - e["b_col"] + e["ig_col"] - m_new)

    for e in inst:
        e["sv"] = jnp.dot(e["s"].astype(BF16), e["vb"], preferred_element_type=F32)
        e["kv"] = lax.dot_general(e["kw"].astype(BF16), e["vb"], (((0,), (0,)), ((), ())),
                                  preferred_element_type=F32)

    for e in inst:
        h, sl = e["h"], e["sl"]
        num = e["sv"] + e["w_inter"] * e["qc"]
        den = (jnp.sum(e["s"], axis=1, keepdims=True)
               + e["w_inter"] * jnp.sum(e["q"] * e["n_prev"], axis=1, keepdims=True))
        hh = num / jnp.maximum(jnp.abs(den), jnp.exp(-e["m_t"]))
        e["c_ref"][0, h] = e["decay"] * e["c_prev"] + e["kv"]
        e["n_ref"][0, h:h + 1, :] = e["decay"] * e["n_prev"] + jnp.sum(e["kw"], axis=0, keepdims=True)
        e["m_ref"][0, h:h + 1, :] = jnp.broadcast_to(e["m_new"], (1, 128))
        e["y_ref"][:, sl] = (e["og"] * _rms(hh, ng_ref[:, sl])).astype(BF16)


def mlstm_group(proj, row0, bsz, t, chunk_len, c0, n0, m0, gate_bias, norm_g, *, groups):
    nc = t // chunk_len
    rb0 = row0 // chunk_len
    per = bsz // groups
    m0p = jnp.pad(jnp.broadcast_to(m0[:, :, None], (bsz, A_HEADS, 128)), ((0, 0), (0, 8 - A_HEADS), (0, 0)))

    in_specs, out_specs, out_shape, args = [], [], [], []
    for g in range(groups):
        def col_spec(width, cb, g=g):
            return pl.BlockSpec((chunk_len, width), lambda b, c: (rb0 + (g * per + b) * nc + c, cb))

        st4 = lambda b, c, g=g: (g * per + b, 0, 0, 0)
        st3 = lambda b, c, g=g: (g * per + b, 0, 0)
        in_specs += [
            col_spec(D_A, CB_Q), col_spec(D_A, CB_K), col_spec(D_A, CB_V), col_spec(D_A, CB_O),
            col_spec(128, CB_KRG),
            pl.BlockSpec((1, A_HEADS, A_HEAD_DIM, A_HEAD_DIM), st4),
            pl.BlockSpec((1, A_HEADS, A_HEAD_DIM), st3),
            pl.BlockSpec((1, 8, 128), st3),
        ]
        args += [proj, proj, proj, proj, proj, c0, n0, m0p]
        out_specs += [
            pl.BlockSpec((chunk_len, D_A), lambda b, c: (b * nc + c, 0)),
            pl.BlockSpec((1, A_HEADS, A_HEAD_DIM, A_HEAD_DIM), lambda b, c: (b, 0, 0, 0)),
            pl.BlockSpec((1, A_HEADS, A_HEAD_DIM), lambda b, c: (b, 0, 0)),
            pl.BlockSpec((1, 8, 128), lambda b, c: (b, 0, 0)),
        ]
        out_shape += [
            jax.ShapeDtypeStruct((per * t, D_A), BF16),
            jax.ShapeDtypeStruct((per, A_HEADS, A_HEAD_DIM, A_HEAD_DIM), F32),
            jax.ShapeDtypeStruct((per, A_HEADS, A_HEAD_DIM), F32),
            jax.ShapeDtypeStruct((per, 8, 128), F32),
        ]
    in_specs += [pl.BlockSpec((1, 128), lambda b, c: (0, 0)), pl.BlockSpec((1, D_A), lambda b, c: (0, 0))]
    args += [gate_bias, norm_g.reshape(1, D_A)]
    outs = pl.pallas_call(
        functools.partial(_mlstm_kernel, groups=groups),
        grid=(per, nc),
        in_specs=in_specs,
        out_specs=out_specs,
        out_shape=out_shape,
        compiler_params=_cparams(("parallel", "arbitrary")),
        name="mlstm",
    )(*args)
    y, c_f, n_f, m_f = (jnp.concatenate(outs[k::4], axis=0) for k in range(4))
    return y, c_f, n_f, m_f[:, :A_HEADS, 0]


CONV_HALO = 32
CONV_ROWS = 32


def _conv_kernel(glu_ref, buf_ref, w_ref, b_ref, lng_ref, lnb_ref, y_ref, nbuf_ref, xp_ref):
    tc = glu_ref.shape[0]
    j = pl.program_id(1)
    hist = CONV_WIDTH - 1

    @pl.when(j == 0)
    def _():
        xp_ref[0:CONV_HALO - hist, :] = jnp.zeros((CONV_HALO - hist, D_B), F32)
        xp_ref[CONV_HALO - hist:CONV_HALO, :] = buf_ref[0]

    @pl.when(j > 0)
    def _():
        xp_ref[0:CONV_HALO, :] = xp_ref[tc:tc + CONV_HALO, :]

    u = glu_ref[:, 0:D_B] * jax.nn.sigmoid(glu_ref[:, D_B:2 * D_B])
    xp_ref[CONV_HALO:CONV_HALO + tc, :] = u

    @pl.when(j == pl.num_programs(1) - 1)
    def _():
        nbuf_ref[0] = xp_ref[CONV_HALO + tc - hist:CONV_HALO + tc, :]

    for r0 in range(0, tc, CONV_ROWS):
        acc = jnp.broadcast_to(b_ref[...], (CONV_ROWS, D_B))
        for tap in range(CONV_WIDTH):
            start = CONV_HALO - hist + r0 + tap
            acc = acc + w_ref[tap:tap + 1, :] * xp_ref[start:start + CONV_ROWS, :]
        mu = jnp.mean(acc, axis=-1, keepdims=True)
        cen = acc - mu
        var = jnp.mean(cen * cen, axis=-1, keepdims=True)
        z = cen * lax.rsqrt(var + EPS) * lng_ref[...] + lnb_ref[...]
        y_ref[r0:r0 + CONV_ROWS, :] = (z * jax.nn.sigmoid(z)).astype(BF16)


def conv_group(proj, row0, bsz, t, tc, buf, w, b, ln_g, ln_b):
    nt = t // tc
    rb0 = row0 // tc
    const2 = lambda bb, j: (0, 0)
    y, nbuf = pl.pallas_call(
        _conv_kernel,
        grid=(bsz, nt),
        in_specs=[
            pl.BlockSpec((tc, 2 * D_B), lambda bb, j: (rb0 + bb * nt + j, CB_GLU)),
            pl.BlockSpec((1, CONV_WIDTH - 1, D_B), lambda bb, j: (bb, 0, 0)),
            pl.BlockSpec((CONV_WIDTH, D_B), const2),
            pl.BlockSpec((1, D_B), const2),
            pl.BlockSpec((1, D_B), const2),
            pl.BlockSpec((1, D_B), const2),
        ],
        out_specs=[
            pl.BlockSpec((tc, D_B), lambda bb, j: (bb * nt + j, 0)),
            pl.BlockSpec((1, CONV_WIDTH - 1, D_B), lambda bb, j: (bb, 0, 0)),
        ],
        out_shape=[
            jax.ShapeDtypeStruct((bsz * t, D_B), BF16),
            jax.ShapeDtypeStruct((bsz, CONV_WIDTH - 1, D_B), F32),
        ],
        scratch_shapes=[pltpu.VMEM((CONV_HALO + tc + CONV_HALO, D_B), F32)],
        compiler_params=_cparams(("parallel", "arbitrary")),
        name="conv_module",
    )(proj, buf, w, b.reshape(1, D_B), ln_g.reshape(1, D_B), ln_b.reshape(1, D_B))
    return y, nbuf


def _mla_prep_kernel(cq_ref, ckv_ref, krg_ref, cos_ref, sin_ref, qg_ref, kvg_ref, wuq_ref, wuk_ref,
                     qlat_ref, qrope_ref, ckv_out_ref, kr_out_ref, ckv_b_ref, kr_b_ref):
    half = C_ROPE // 2
    cq = _rms(cq_ref[...], qg_ref[...]).astype(BF16)
    qh = jnp.dot(cq, wuq_ref[...], preferred_element_type=F32)
    cos = cos_ref[...]
    sin = sin_ref[...]
    x1 = qh[:, D_C:D_C + C_HEADS * half]
    x2 = qh[:, D_C + C_HEADS * half:D_C + 2 * C_HEADS * half]
    r1 = (x1 * cos - x2 * sin) * Q_SCALE
    r2 = (x2 * cos + x1 * sin) * Q_SCALE
    for h in range(C_HEADS):
        qn = qh[:, h * C_NOPE:(h + 1) * C_NOPE].astype(BF16)
        ql = lax.dot_general(qn, wuk_ref[h], (((1,), (1,)), ((), ())), preferred_element_type=F32)
        qlat_ref[h] = (ql * Q_SCALE).astype(BF16)
        qrope_ref[h] = jnp.concatenate(
            [r1[:, h * half:(h + 1) * half], r2[:, h * half:(h + 1) * half]], axis=1).astype(BF16)
    ckv = _rms(ckv_ref[...], kvg_ref[...])
    ckv_out_ref[...] = ckv
    ckv_b_ref[...] = ckv.astype(BF16)
    k1 = krg_ref[:, 0:half]
    k2 = krg_ref[:, half:C_ROPE]
    c1 = cos[:, 0:half]
    s1 = sin[:, 0:half]
    kr = jnp.concatenate([k1 * c1 - k2 * s1, k2 * c1 + k1 * s1], axis=1)
    kr_out_ref[...] = kr
    kr_b_ref[...] = kr.astype(BF16)


def mla_prep(proj, cos_t, sin_t, q_norm_g, kv_norm_g, w_uq_r, w_uk_b, *, tq=256):
    n = proj.shape[0]
    const2 = lambda i: (0, 0)
    return pl.pallas_call(
        _mla_prep_kernel,
        grid=(n // tq,),
        in_specs=[
            pl.BlockSpec((tq, Q_RANK), lambda i: (i, CB_CQ)),
            pl.BlockSpec((tq, KV_RANK), lambda i: (i, CB_CKV)),
            pl.BlockSpec((tq, 128), lambda i: (i, CB_KRG)),
            pl.BlockSpec((tq, C_HEADS * C_ROPE // 2), lambda i: (i, 0)),
            pl.BlockSpec((tq, C_HEADS * C_ROPE // 2), lambda i: (i, 0)),
            pl.BlockSpec((1, Q_RANK), const2),
            pl.BlockSpec((1, KV_RANK), const2),
            pl.BlockSpec((Q_RANK, C_HEADS * (C_NOPE + C_ROPE)), const2),
            pl.BlockSpec((C_HEADS, KV_RANK, C_NOPE), lambda i: (0, 0, 0)),
        ],
        out_specs=[
            pl.BlockSpec((C_HEADS, tq, KV_RANK), lambda i: (0, i, 0)),
            pl.BlockSpec((C_HEADS, tq, C_ROPE), lambda i: (0, i, 0)),
            pl.BlockSpec((tq, KV_RANK), lambda i: (i, 0)),
            pl.BlockSpec((tq, C_ROPE), lambda i: (i, 0)),
            pl.BlockSpec((tq, KV_RANK), lambda i: (i, 0)),
            pl.BlockSpec((tq, C_ROPE), lambda i: (i, 0)),
        ],
        out_shape=[
            jax.ShapeDtypeStruct((C_HEADS, n, KV_RANK), BF16),
            jax.ShapeDtypeStruct((C_HEADS, n, C_ROPE), BF16),
            jax.ShapeDtypeStruct((n, KV_RANK), F32),
            jax.ShapeDtypeStruct((n, C_ROPE), F32),
            jax.ShapeDtypeStruct((n, KV_RANK), BF16),
            jax.ShapeDtypeStruct((n, C_ROPE), BF16),
        ],
        compiler_params=_cparams(("parallel",)),
        name="mla_prep",
    )(proj, proj, proj, cos_t, sin_t, q_norm_g.reshape(1, Q_RANK), kv_norm_g.reshape(1, KV_RANK), w_uq_r, w_uk_b)


ATT_SLAB = 256
LANES = 128
_NT = (((1,), (1,)), ((), ()))


ATT_GROUP = 8


def _attn_slabs(slabs, kcb, krb, bias, m_ref, l_ref, acc_ref):
    for g0 in range(0, len(slabs), ATT_GROUP):
        group = slabs[g0:g0 + ATT_GROUP]
        work = []
        for q, qr, rows in group:
            s = (lax.dot_general(q(), kcb, _NT, preferred_element_type=F32)
                 + lax.dot_general(qr(), krb, _NT, preferred_element_type=F32))
            work.append(dict(rows=rows, s=s))
        for w in work:
            s = w["s"] if bias is None else w["s"] + bias
            tiles = [s[:, t * LANES:(t + 1) * LANES] for t in range(s.shape[1] // LANES)]
            m_prev = m_ref[w["rows"], :]
            m_new = jnp.maximum(m_prev, jnp.max(functools.reduce(jnp.maximum, tiles), axis=1, keepdims=True))
            w["alpha"] = jnp.exp2(m_prev - m_new)
            ps = [jnp.exp2(t - m_new) for t in tiles]
            w["m_new"] = m_new
            w["l_new"] = w["alpha"] * l_ref[w["rows"], :] + functools.reduce(jnp.add, ps)
            w["p"] = jnp.concatenate(ps, axis=1).astype(BF16)
        for w in work:
            w["pv"] = jnp.dot(w["p"], kcb, preferred_element_type=F32)
        for w in work:
            rows = w["rows"]
            acc_ref[rows, :] = jnp.concatenate([w["alpha"]] * (KV_RANK // LANES), axis=1) * acc_ref[rows, :] + w["pv"]
            l_ref[rows, :] = w["l_new"]
            m_ref[rows, :] = w["m_new"]


def _attn_finish(acc_ref, l_ref, wuv_ref, y_ref, tq):
    for h in range(C_HEADS):
        rows = slice(h * tq, (h + 1) * tq)
        o = acc_ref[rows, :] / jnp.sum(l_ref[rows, :], axis=1, keepdims=True)
        y_ref[:, h * C_VDIM:(h + 1) * C_VDIM] = jnp.dot(
            o.astype(BF16), wuv_ref[h], preferred_element_type=F32).astype(BF16)


def _attn_init(m_ref, l_ref, acc_ref):
    m_ref[...] = jnp.full(m_ref.shape, NEG_INF, F32)
    l_ref[...] = jnp.zeros(l_ref.shape, F32)
    acc_ref[...] = jnp.zeros(acc_ref.shape, F32)


def _attn_prompt_kernel(ql_ref, qr_ref, k_ref, kr_ref, wuv_ref, y_ref, m_ref, l_ref, acc_ref, *, kb):
    tq = ql_ref.shape[1]
    i = pl.program_id(1)
    _attn_init(m_ref, l_ref, acc_ref)

    def block(start, bias):
        kcb = k_ref[pl.ds(start, kb), :]
        krb = kr_ref[pl.ds(start, kb), :]
        slabs = [(functools.partial(lambda h: ql_ref[h], h), functools.partial(lambda h: qr_ref[h], h),
                  slice(h * tq, (h + 1) * tq)) for h in range(C_HEADS)]
        _attn_slabs(slabs, kcb, krb, bias, m_ref, l_ref, acc_ref)

    diag = (i * tq) // kb

    def body(j, carry):
        block(pl.multiple_of(j * kb, kb), None)
        return carry

    lax.fori_loop(0, diag, body, 0)
    start = pl.multiple_of(diag * kb, kb)
    qchunk = (i * tq + lax.broadcasted_iota(jnp.int32, (tq, kb), 0)) // CHUNK
    kchunk = (start + lax.broadcasted_iota(jnp.int32, (tq, kb), 1)) // CHUNK
    block(start, jnp.where(kchunk <= qchunk, 0.0, NEG_INF))
    _attn_finish(acc_ref, l_ref, wuv_ref, y_ref, tq)


def attn_prompt(q_lat, q_rope, ckv_b, kr_b, w_uv_b, bsz, t, *, tq=ATT_SLAB, kb=512):
    nq = t // tq
    rows = C_HEADS * tq
    return pl.pallas_call(
        functools.partial(_attn_prompt_kernel, kb=kb),
        grid=(bsz, nq),
        in_specs=[
            pl.BlockSpec((C_HEADS, tq, KV_RANK), lambda b, i: (0, b * nq + i, 0)),
            pl.BlockSpec((C_HEADS, tq, C_ROPE), lambda b, i: (0, b * nq + i, 0)),
            pl.BlockSpec((t, KV_RANK), lambda b, i: (b, 0)),
            pl.BlockSpec((t, C_ROPE), lambda b, i: (b, 0)),
            pl.BlockSpec((C_HEADS, KV_RANK, C_VDIM), lambda b, i: (0, 0, 0)),
        ],
        out_specs=pl.BlockSpec((tq, D_C), lambda b, i: (b * nq + i, 0)),
        out_shape=jax.ShapeDtypeStruct((bsz * t, D_C), BF16),
        scratch_shapes=[pltpu.VMEM((rows, LANES), F32), pltpu.VMEM((rows, LANES), F32),
                        pltpu.VMEM((rows, KV_RANK), F32)],
        compiler_params=_cparams(("parallel", "arbitrary")),
        name="attn_prompt",
    )(q_lat, q_rope, ckv_b, kr_b, w_uv_b)


def _attn_sample_kernel(ql_ref, qr_ref, k_ref, kr_ref, ck_ref, ckr_ref, wuv_ref, y_ref, m_ref, l_ref, acc_ref,
                        *, kb):
    tq = ql_ref.shape[1]
    rows_total = C_HEADS * tq
    hps = ATT_SLAB // tq
    _attn_init(m_ref, l_ref, acc_ref)
    past = ck_ref.shape[1]

    def block(kcb, krb, bias):
        slabs = [(functools.partial(lambda i: ql_ref[i * hps:(i + 1) * hps].reshape(ATT_SLAB, KV_RANK), sidx),
                  functools.partial(lambda i: qr_ref[i * hps:(i + 1) * hps].reshape(ATT_SLAB, C_ROPE), sidx),
                  slice(sidx * ATT_SLAB, (sidx + 1) * ATT_SLAB)) for sidx in range(rows_total // ATT_SLAB)]
        _attn_slabs(slabs, kcb, krb, bias, m_ref, l_ref, acc_ref)

    def body(j, carry):
        start = pl.multiple_of(j * kb, kb)
        block(ck_ref[0, pl.ds(start, kb), :].astype(BF16), ckr_ref[0, pl.ds(start, kb), :].astype(BF16), None)
        return carry

    lax.fori_loop(0, past // kb, body, 0)
    pad = LANES - tq
    kcb = jnp.concatenate([k_ref[...], jnp.zeros((pad, KV_RANK), BF16)], axis=0)
    krb = jnp.concatenate([kr_ref[...], jnp.zeros((pad, C_ROPE), BF16)], axis=0)
    bias = jnp.where(lax.broadcasted_iota(jnp.int32, (1, LANES), 1) < tq, 0.0, NEG_INF)
    block(kcb, krb, bias)
    _attn_finish(acc_ref, l_ref, wuv_ref, y_ref, tq)


def attn_sample(q_lat, q_rope, ckv_b, kr_b, cache_lat, cache_rope, layer, w_uv_b, row0, bsz, t, *, kb=1024):
    rb0 = row0 // t
    past = cache_lat.shape[2]
    rows = C_HEADS * t
    return pl.pallas_call(
        functools.partial(_attn_sample_kernel, kb=kb),
        grid=(bsz,),
        in_specs=[
            pl.BlockSpec((C_HEADS, t, KV_RANK), lambda b: (0, rb0 + b, 0)),
            pl.BlockSpec((C_HEADS, t, C_ROPE), lambda b: (0, rb0 + b, 0)),
            pl.BlockSpec((t, KV_RANK), lambda b: (rb0 + b, 0)),
            pl.BlockSpec((t, C_ROPE), lambda b: (rb0 + b, 0)),
            pl.BlockSpec((None, 1, past, KV_RANK), lambda b: (layer, b, 0, 0)),
            pl.BlockSpec((None, 1, past, C_ROPE), lambda b: (layer, b, 0, 0)),
            pl.BlockSpec((C_HEADS, KV_RANK, C_VDIM), lambda b: (0, 0, 0)),
        ],
        out_specs=pl.BlockSpec((t, D_C), lambda b: (b, 0)),
        out_shape=jax.ShapeDtypeStruct((bsz * t, D_C), BF16),
        scratch_shapes=[pltpu.VMEM((rows, LANES), F32), pltpu.VMEM((rows, LANES), F32),
                        pltpu.VMEM((rows, KV_RANK), F32)],
        compiler_params=_cparams(("parallel",)),
        name="attn_sample",
    )(q_lat, q_rope, ckv_b, kr_b, cache_lat, cache_rope, w_uv_b)


def _oproj_kernel(x_ref, ya_ref, yb_ref, yc_ref, wa_ref, wb_ref, wc_ref, o_ref):
    acc = x_ref[...]
    acc = acc + jnp.dot(ya_ref[...], wa_ref[...], preferred_element_type=F32)
    acc = acc + jnp.dot(yb_ref[...], wb_ref[...], preferred_element_type=F32)
    acc = acc + jnp.dot(yc_ref[...], wc_ref[...], preferred_element_type=F32)
    o_ref[...] = acc


def out_proj(x, ya, yb, yc, w_out_all, layer, *, tm=1024, tn=512):
    n, d = x.shape
    return pl.pallas_call(
        _oproj_kernel,
        grid=(n // tm, d // tn),
        in_specs=[
            pl.BlockSpec((tm, tn), lambda i, j: (i, j)),
            pl.BlockSpec((tm, D_A), lambda i, j: (i, 0)),
            pl.BlockSpec((tm, D_B), lambda i, j: (i, 0)),
            pl.BlockSpec((tm, D_C), lambda i, j: (i, 0)),
            pl.BlockSpec((None, D_A, tn), lambda i, j: (layer, 0, j)),
            pl.BlockSpec((None, D_B, tn), lambda i, j: (layer, 1, j)),
            pl.BlockSpec((None, D_C, tn), lambda i, j: (layer, 1, j)),
        ],
        out_specs=pl.BlockSpec((tm, tn), lambda i, j: (i, j)),
        out_shape=jax.ShapeDtypeStruct((n, d), F32),
        compiler_params=_cparams(("parallel", "arbitrary")),
        name="out_proj",
    )(x, ya, yb, yc, w_out_all, w_out_all, w_out_all)


STEP_SCALE = 2.0 ** 64


def _extract_max(cur):
    m = jnp.max(cur, axis=0, keepdims=True)
    return m, jnp.where(cur == m, NEG_INF, cur)


def _peer_select_kernel(q_ref, keys_ref, se_ref, thr_ref, e1_ref):
    for h in range(PEER_HEADS):
        st = []
        top = []
        for z in range(2):
            qhz = q_ref[:, (2 * h + z) * N_KEYS:(2 * h + z + 1) * N_KEYS].astype(BF16)
            s = lax.dot_general(keys_ref[h, z], qhz, (((1,), (1,)), ((), ())),
                                preferred_element_type=F32)
            st.append(s)
            cur = s
            vals = []
            for _ in range(PEER_TOPK + 1):
                m, cur = _extract_max(cur)
                vals.append(m)
            top.append(vals)
        v1, v2 = top
        v2_16 = jnp.concatenate(v2[:PEER_TOPK], axis=0)
        v2_8 = v2_16[:8]
        rowid = lax.broadcasted_iota(jnp.int32, v2_8.shape, 0)
        pieces = [v1[0] + v2_16, v1[1] + v2_8]
        for a in range(2, 8):
            pieces.append(jnp.where(rowid < PEER_TOPK // (a + 1), v1[a] + v2_8, NEG_INF))
        pieces.append(jnp.concatenate(v1[8:PEER_TOPK], axis=0) + v2[0])
        cur = jnp.concatenate(pieces, axis=0)
        best, cur = _extract_max(cur)
        zsum = jnp.ones_like(best)
        m = best
        for _ in range(PEER_TOPK - 1):
            m, cur = _extract_max(cur)
            zsum = zsum + jnp.exp(m - best)
        nxt = jnp.maximum(jnp.max(cur, axis=0, keepdims=True),
                          jnp.maximum(v1[PEER_TOPK] + v2[0], v1[0] + v2[PEER_TOPK]))
        tau = 0.5 * (m + nxt)
        s2 = st[1] * STEP_SCALE
        e2 = jnp.exp(st[1] - top[1][0]) / zsum
        for t in range(s2.shape[1] // LANES):
            cols = slice(t * LANES, (t + 1) * LANES)
            se_ref[h, t, :, 0] = s2[:, cols].reshape(N_KEYS // 8, 8, LANES)
            se_ref[h, t, :, 1] = e2[:, cols].reshape(N_KEYS // 8, 8, LANES)
        thr_ref[h] = (tau - st[0]) * STEP_SCALE
        e1_ref[h] = jnp.exp(st[0] - top[0][0])


def peer_select(q, keys_b, *, tp=256):
    n = q.shape[0]
    shp = jax.ShapeDtypeStruct((PEER_HEADS, N_KEYS, n), F32)
    spec = pl.BlockSpec((PEER_HEADS, N_KEYS, tp), lambda i: (0, 0, i))
    tshp = jax.ShapeDtypeStruct((PEER_HEADS, n // LANES, N_KEYS // 8, 2, 8, LANES), F32)
    tspec = pl.BlockSpec((PEER_HEADS, tp // LANES, N_KEYS // 8, 2, 8, LANES), lambda i: (0, i, 0, 0, 0, 0))
    return pl.pallas_call(
        _peer_select_kernel,
        grid=(n // tp,),
        in_specs=[
            pl.BlockSpec((tp, PEER_HEADS * D_KEY), lambda i: (i, 0)),
            pl.BlockSpec((PEER_HEADS, 2, N_KEYS, D_KEY // 2), lambda i: (0, 0, 0, 0)),
        ],
        out_specs=[tspec, spec, spec],
        out_shape=[tshp, shp, shp],
        compiler_params=_cparams(("parallel",)),
        name="peer_select",
    )(q, keys_b)


PEER_CHUNK = 256


def _peer_dense_kernel(x_ref, xnt_ref, se_ref, thr_ref, e1_ref, u_ref, v_ref, o_ref, wt_ref, *, eb):
    j = pl.program_id(1)
    nchunk = eb // PEER_CHUNK
    per = PEER_CHUNK // N_KEYS

    @pl.when(j == 0)
    def _():
        o_ref[...] = x_ref[...]

    def scores(c):
        return jnp.dot(u_ref[c * PEER_CHUNK:(c + 1) * PEER_CHUNK, :], xnt_ref[...], preferred_element_type=F32)

    def gate_chunk(c, at):
        for ii in range(per):
            row = c * per + ii
            thr_rows = [thr_ref[h, row:row + 1, :] for h in range(PEER_HEADS)]
            e1_rows = [e1_ref[h, row:row + 1, :] for h in range(PEER_HEADS)]
            for t in range(xnt_ref.shape[1] // LANES):
                cols = slice(t * LANES, (t + 1) * LANES)
                g = None
                for h in range(PEER_HEADS):
                    s2 = se_ref[h, t, :, 0].reshape(N_KEYS, LANES)
                    e2 = se_ref[h, t, :, 1].reshape(N_KEYS, LANES)
                    term = jnp.maximum(jnp.minimum(s2 - thr_rows[h][:, cols], e2), 0.0) * e1_rows[h][:, cols]
                    g = term if g is None else g + term
                a = at[ii * N_KEYS:(ii + 1) * N_KEYS, cols]
                act = 0.5 * a * (1.0 + lax.erf(a * (2.0 ** -0.5)))
                wt_ref[row * N_KEYS:(row + 1) * N_KEYS, cols] = (g * act).astype(BF16)

    at_next = scores(0)
    for c in range(nchunk):
        at = at_next
        if c + 1 < nchunk:
            at_next = scores(c + 1)
        gate_chunk(c, at)
        rows = slice(c * PEER_CHUNK, (c + 1) * PEER_CHUNK)
        o_ref[...] += lax.dot_general(wt_ref[rows, :], v_ref[rows, :], (((0,), (0,)), ((), ())),
                                      preferred_element_type=F32)


def peer_dense(x, xnt, se, thr, e1, u_all, v_all, layer, *, tp=512, eb=1024):
    n, d = x.shape
    nb = N_EXPERTS // eb
    once = pl.Buffered(1)
    gspec = pl.BlockSpec((PEER_HEADS, tp // LANES, N_KEYS // 8, 2, 8, LANES), lambda i, j: (0, i, 0, 0, 0, 0),
                         pipeline_mode=once)
    rspec = pl.BlockSpec((PEER_HEADS, eb // N_KEYS, tp), lambda i, j: (0, j, i))
    return pl.pallas_call(
        functools.partial(_peer_dense_kernel, eb=eb),
        grid=(n // tp, nb),
        in_specs=[
            pl.BlockSpec((tp, d), lambda i, j: (i, 0), pipeline_mode=once),
            pl.BlockSpec((d, tp), lambda i, j: (0, i), pipeline_mode=once),
            gspec, rspec, rspec,
            pl.BlockSpec((None, eb, d), lambda i, j: (layer, j, 0)),
            pl.BlockSpec((None, eb, d), lambda i, j: (layer, j, 0)),
        ],
        out_specs=pl.BlockSpec((tp, d), lambda i, j: (i, 0)),
        out_shape=jax.ShapeDtypeStruct((n, d), F32),
        scratch_shapes=[pltpu.VMEM((eb, tp), BF16)],
        compiler_params=_cparams(("parallel", "arbitrary")),
        name="peer_dense",
    )(x, xnt, se, thr, e1, u_all, v_all)


def _final_norm_kernel(x_ref, g_ref, o_ref):
    o_ref[...] = _rms(x_ref[...], g_ref[...])


def final_norm(x, g, *, tm=512):
    n, d = x.shape
    spec = pl.BlockSpec((tm, d), lambda i: (i, 0))
    return pl.pallas_call(
        _final_norm_kernel,
        grid=(n // tm,),
        in_specs=[spec, pl.BlockSpec((1, d), lambda i: (0, 0))],
        out_specs=spec,
        out_shape=jax.ShapeDtypeStruct((n, d), F32),
        compiler_params=_cparams(("parallel",)),
        name="final_norm",
    )(x, g.reshape(1, d))


def _prep_w_in(w_in):
    qa, ka, va, oa, ig, fg, glu, cq, ckv, kr = jnp.split(w_in, SPLIT_IDX, axis=-1)
    pad = jnp.zeros(w_in.shape[:-1] + (PROJ_COLS - IN_COLS,), w_in.dtype)
    return jnp.concatenate([qa, ka, va, oa, glu, cq, ckv, kr, ig, fg, pad], axis=-1).astype(BF16)


def _prep_w_uq(w_uq):
    w = w_uq.reshape(w_uq.shape[:-1] + (C_HEADS, C_NOPE + C_ROPE))
    half = C_ROPE // 2
    flat = lambda a: a.reshape(a.shape[:-2] + (-1,))
    return jnp.concatenate(
        [flat(w[..., :C_NOPE]), flat(w[..., C_NOPE:C_NOPE + half]), flat(w[..., C_NOPE + half:])],
        axis=-1).astype(BF16)


def _stream_groups(bsz, want):
    return max(g for g in range(1, want + 1) if bsz % g == 0)


def _rope_tables(pos):
    half = C_ROPE // 2
    inv = ROPE_BASE ** (-jnp.arange(half, dtype=jnp.float32) / half)
    ang = pos.astype(jnp.float32)[:, None] * inv[None, :]
    return jnp.tile(jnp.cos(ang), (1, C_HEADS)), jnp.tile(jnp.sin(ang), (1, C_HEADS))


def kernel(x_prompt, x_sample, state_mlstm_C, state_mlstm_n, state_mlstm_m, state_conv, cache_kv_latent, cache_k_rope, norm1_g, w_in, b_gate, mlstm_norm_g, conv_w, conv_b, conv_ln_g, conv_ln_b, q_norm_g, kv_norm_g, w_uq, w_uk, w_uv, w_out, norm2_g, peer_wq, peer_keys, peer_u, peer_v, final_g):
    bp, sp, _ = x_prompt.shape
    bs, ts, _ = x_sample.shape
    past = cache_kv_latent.shape[2]
    n_p = bp * sp
    n_s = bs * ts

    x = jnp.concatenate([x_prompt.reshape(n_p, D_MODEL), x_sample.reshape(n_s, D_MODEL)], axis=0)
    pos = jnp.concatenate([jnp.tile(jnp.arange(sp), bp), jnp.tile(past + jnp.arange(ts), bs)])
    cos_t, sin_t = _rope_tables(pos)

    w_in_b = _prep_w_in(w_in)
    w_uq_b = _prep_w_uq(w_uq)
    w_uk_b = w_uk.astype(BF16)
    w_uv_b = w_uv.astype(BF16)
    w_out_b = w_out.astype(BF16)
    peer_wq_b = peer_wq.astype(BF16)
    peer_keys_b = peer_keys.astype(BF16)
    peer_u_b = peer_u.astype(BF16)
    peer_v_b = peer_v.astype(BF16)
    gate_bias = jnp.pad(b_gate, ((0, 0), (GATE_LANE, 128 - GATE_LANE - 2 * A_HEADS)))[:, None, :]

    zc = jnp.zeros((bp, A_HEADS, A_HEAD_DIM, A_HEAD_DIM), F32)
    zn = jnp.zeros((bp, A_HEADS, A_HEAD_DIM), F32)
    zm = jnp.zeros((bp, A_HEADS), F32)
    zbuf = jnp.zeros((bp, CONV_WIDTH - 1, D_B), F32)

    p_st = [[] for _ in range(6)]
    s_st = [[] for _ in range(6)]
    for l in range(DEPTH):
        proj, _ = norm_matmul(x, norm1_g[l], w_in_b, l)
        ya_p, c_p, nn_p, m_p = mlstm_group(proj, 0, bp, sp, CHUNK, zc, zn, zm, gate_bias[l], mlstm_norm_g[l],
                                           groups=_stream_groups(bp, 2))
        ya_s, c_s, nn_s, m_s = mlstm_group(proj, n_p, bs, ts, ts, state_mlstm_C[l], state_mlstm_n[l],
                                           state_mlstm_m[l], gate_bias[l], mlstm_norm_g[l],
                                           groups=_stream_groups(bs, 4))
        yb_p, buf_p = conv_group(proj, 0, bp, sp, 256, zbuf, conv_w[l], conv_b[l], conv_ln_g[l], conv_ln_b[l])
        yb_s, buf_s = conv_group(proj, n_p, bs, ts, ts, state_conv[l], conv_w[l], conv_b[l], conv_ln_g[l],
                                 conv_ln_b[l])
        q_lat, q_rope, ckv, kr, ckv_b, kr_b = mla_prep(proj, cos_t, sin_t, q_norm_g[l], kv_norm_g[l], w_uq_b[l],
                                                       w_uk_b[l])
        yc_p = attn_prompt(q_lat, q_rope, ckv_b, kr_b, w_uv_b[l], bp, sp)
        yc_s = attn_sample(q_lat, q_rope, ckv_b, kr_b, cache_kv_latent, cache_k_rope, l, w_uv_b[l], n_p, bs, ts)
        ya = jnp.concatenate([ya_p, ya_s], axis=0)
        yb = jnp.concatenate([yb_p, yb_s], axis=0)
        yc = jnp.concatenate([yc_p, yc_s], axis=0)
        x = out_proj(x, ya, yb, yc, w_out_b, l)
        q, xnt = norm_matmul(x, norm2_g[l], peer_wq_b, l, emit_xn=True)
        se, thr, e1 = peer_select(q, peer_keys_b[l])
        x = peer_dense(x, xnt, se, thr, e1, peer_u_b, peer_v_b, l)

        for lst, vals in ((p_st, (c_p, nn_p, m_p, buf_p, ckv[:n_p].reshape(bp, sp, KV_RANK),
                                  kr[:n_p].reshape(bp, sp, C_ROPE))),
                          (s_st, (c_s, nn_s, m_s, buf_s, ckv[n_p:].reshape(bs, ts, KV_RANK),
                                  kr[n_p:].reshape(bs, ts, C_ROPE)))):
            for i in range(6):
                lst[i].append(vals[i])

    y = final_norm(x, final_g)
    y_prompt = y[:n_p].reshape(bp, sp, D_MODEL)
    y_sample = y[n_p:].reshape(bs, ts, D_MODEL)
    return (y_prompt, y_sample, *[jnp.stack(a) for a in p_st], *[jnp.stack(a) for a in s_st])
```
